```python
import math
import jax, jax.numpy as jnp
from jax import lax
import numpy as np

D_MODEL = 1024
BATCH = 8
SEQ = 4096
DEPTH = 2
DEC_BATCH = 32
DEC_SEQ = 1
PAST_LEN = 16384
PAGE_SIZE = 128

N_MIXERS = 2
N_HEADS = 16
HEAD_DIM = D_MODEL // N_HEADS
D_FF = 2816
CONV_W = 3
FFN_CONV_W = 3
Q_BLOCK = 128
NORM_EPS = 1e-6
N_MOD = 6
SB_BIAS_INIT = -6.0
N_ATTN = (DEPTH + N_MIXERS - 1) // N_MIXERS
N_CONV = DEPTH // N_MIXERS

kernel_name = "stickbreak_shortconv_hybrid_step"


def rms_norm(x, g):
    xf = x.astype(jnp.float32)
    y = xf * lax.rsqrt(jnp.mean(xf * xf, axis=-1, keepdims=True) + NORM_EPS)
    return (y * g.astype(jnp.float32)).astype(x.dtype)


def ada_mod(c, w, b):
    m = jax.nn.silu(c) @ w + b
    return [t[:, None, :] for t in jnp.split(m, N_MOD, axis=-1)]


def causal_dwconv(u, prev, w):
    width = w.shape[0]
    t_len = u.shape[1]
    up = jnp.concatenate([prev.astype(u.dtype), u], axis=1)
    y = w[0] * up[:, 0:t_len]
    for i in range(1, width):
        y = y + w[i] * up[:, i:i + t_len]
    return y, up[:, t_len:]


def stick_breaking_block(q, q_pos, k, v, k_pos, bias):
    z = jnp.einsum('bqhd,bkhd->bhqk', q.astype(jnp.float32), k.astype(jnp.float32)) * (HEAD_DIM ** -0.5)
    z = z + bias.astype(jnp.float32)[None, :, None, None]
    mask = k_pos[None, :] < q_pos[:, None]
    log_fail = jnp.where(mask, jax.nn.log_sigmoid(-z), 0.0)
    later = lax.cumsum(log_fail, axis=3, reverse=True) - log_fail
    a = jnp.where(mask, jnp.exp(jax.nn.log_sigmoid(z) + later), 0.0)
    return jnp.einsum('bhqk,bkhd->bqhd', a, v.astype(jnp.float32)).astype(q.dtype)


def sb_attention(h, w_qkv, w_o, bias, past_k, past_v):
    bn, t_len, _ = h.shape
    qkv = (h @ w_qkv).reshape(bn, t_len, 3, N_HEADS, HEAD_DIM)
    q, k, v = qkv[:, :, 0], qkv[:, :, 1], qkv[:, :, 2]
    if past_k is None:
        pos = jnp.arange(t_len)
        nb = t_len // Q_BLOCK
        qb = q.reshape(bn, nb, Q_BLOCK, N_HEADS, HEAD_DIM).transpose(1, 0, 2, 3, 4)
        pb = pos.reshape(nb, Q_BLOCK)
        ob = lax.map(lambda a: stick_breaking_block(a[0], a[1], k, v, pos, bias), (qb, pb))
        o = ob.transpose(1, 0, 2, 3, 4).reshape(bn, t_len, D_MODEL)
    else:
        p_len = past_k.shape[1]
        kk = jnp.concatenate([past_k.astype(k.dtype), k], axis=1)
        vv = jnp.concatenate([past_v.astype(v.dtype), v], axis=1)
        k_pos = jnp.arange(p_len + t_len)
        q_pos = p_len + jnp.arange(t_len)
        o = stick_breaking_block(q, q_pos, kk, vv, k_pos, bias).reshape(bn, t_len, D_MODEL)
    return o @ w_o, k, v


def short_conv_mixer(h, prev, w_in, conv_w, w_out):
    b_gate, c_gate, u = jnp.split(h @ w_in, 3, axis=-1)
    y, new_prev = causal_dwconv(c_gate * u, prev, conv_w)
    return (b_gate * y) @ w_out, new_prev


def conv_ffn(h, prev, w_up, conv_w, conv_b, w_down):
    g, u = jnp.split(h @ w_up, 2, axis=-1)
    a, new_prev = causal_dwconv(g, prev, conv_w)
    return (jax.nn.silu(a + conv_b) * u) @ w_down, new_prev


def run_group(x, c, cache_k, cache_v, page_table, conv_prev, ffn_prev,
              ada_w, ada_b, norm_mix, norm_ffn, norm_out,
              attn_w_qkv, attn_w_o, attn_bias, sc_w_in, sc_conv_w, sc_w_out,
              ffn_w_up, ffn_conv_w, ffn_conv_b, ffn_w_down):
    new_k, new_v, new_conv, new_ffn = [], [], [], []
    for i in range(DEPTH):
        sh1, sc1, g1, sh2, sc2, g2 = ada_mod(c, ada_w[i], ada_b[i])
        h = rms_norm(x, norm_mix[i]) * (1 + sc1) + sh1
        if i % N_MIXERS == 0:
            a = i // N_MIXERS
            if page_table is None:
                pk, pv = None, None
            else:
                db, n_pages = page_table.shape
                pk = cache_k[a][page_table].reshape(db, n_pages * PAGE_SIZE, N_HEADS, HEAD_DIM)
                pv = cache_v[a][page_table].reshape(db, n_pages * PAGE_SIZE, N_HEADS, HEAD_DIM)
            out, k, v = sb_attention(h, attn_w_qkv[a], attn_w_o[a], attn_bias[a], pk, pv)
            new_k.append(k)
            new_v.append(v)
        else:
            j = i // N_MIXERS
            out, st = short_conv_mixer(h, conv_prev[j], sc_w_in[j], sc_conv_w[j], sc_w_out[j])
            new_conv.append(st)
        x = x + g1 * out
        h = rms_norm(x, norm_ffn[i]) * (1 + sc2) + sh2
        out, st = conv_ffn(h, ffn_prev[i], ffn_w_up[i], ffn_conv_w[i], ffn_conv_b[i], ffn_w_down[i])
        new_ffn.append(st)
        x = x + g2 * out
    y = rms_norm(x, norm_out)
    return y, jnp.stack(new_k), jnp.stack(new_v), jnp.stack(new_conv), jnp.stack(new_ffn)


def setup_inputs(seed: int = 0) -> dict:
    key = jax.random.key(seed)
    ks = jax.random.split(key, 24)
    n_pages = PAST_LEN // PAGE_SIZE
    n_pool = (DEC_BATCH * n_pages * 5) // 4
    f32 = jnp.float32
    nrm = lambda k, shape, s=1.0: jax.random.normal(k, shape, f32) * s
    page_table = jax.random.permutation(ks[0], n_pool)[:DEC_BATCH * n_pages].reshape(DEC_BATCH, n_pages).astype(jnp.int32)
    return {
        "x_prompt": nrm(ks[1], (BATCH, SEQ, D_MODEL)),
        "x_sample": nrm(ks[2], (DEC_BATCH, DEC_SEQ, D_MODEL)),
        "cache_k": nrm(ks[3], (N_ATTN, n_pool, PAGE_SIZE, N_HEADS, HEAD_DIM)),
        "cache_v": nrm(ks[4], (N_ATTN, n_pool, PAGE_SIZE, N_HEADS, HEAD_DIM)),
        "state_conv": nrm(ks[5], (N_CONV, DEC_BATCH, CONV_W - 1, D_MODEL)),
        "state_ffn": nrm(ks[6], (DEPTH, DEC_BATCH, FFN_CONV_W - 1, D_FF)),
        "page_table": page_table,
        "c_prompt": nrm(ks[7], (BATCH, D_MODEL)),
        "c_sample": nrm(ks[8], (DEC_BATCH, D_MODEL)),
        "ada_w": nrm(ks[9], (DEPTH, D_MODEL, N_MOD * D_MODEL), D_MODEL ** -0.5),
        "ada_b": nrm(ks[10], (DEPTH, N_MOD * D_MODEL), 0.02),
        "norm_mix": 1.0 + nrm(ks[11], (DEPTH, D_MODEL), 0.02),
        "norm_ffn": 1.0 + nrm(ks[12], (DEPTH, D_MODEL), 0.02),
        "norm_out": 1.0 + nrm(ks[13], (D_MODEL,), 0.02),
        "attn_w_qkv": nrm(ks[14], (N_ATTN, D_MODEL, 3 * D_MODEL), D_MODEL ** -0.5),
        "attn_w_o": nrm(ks[15], (N_ATTN, D_MODEL, D_MODEL), D_MODEL ** -0.5),
        "attn_bias": SB_BIAS_INIT + nrm(ks[23], (N_ATTN, N_HEADS), 0.1),
        "sc_w_in": nrm(ks[16], (N_CONV, D_MODEL, 3 * D_MODEL), D_MODEL ** -0.5),
        "sc_conv_w": nrm(ks[17], (N_CONV, CONV_W, D_MODEL), CONV_W ** -0.5),
        "sc_w_out": nrm(ks[18], (N_CONV, D_MODEL, D_MODEL), D_MODEL ** -0.5),
        "ffn_w_up": nrm(ks[19], (DEPTH, D_MODEL, 2 * D_FF), D_MODEL ** -0.5),
        "ffn_conv_w": nrm(ks[20], (DEPTH, FFN_CONV_W, D_FF), FFN_CONV_W ** -0.5),
        "ffn_conv_b": nrm(ks[21], (DEPTH, D_FF), 0.02),
        "ffn_w_down": nrm(ks[22], (DEPTH, D_FF, D_MODEL), D_FF ** -0.5),
    }


def reference(x_prompt, x_sample, cache_k, cache_v, state_conv, state_ffn, page_table,
              c_prompt, c_sample, ada_w, ada_b, norm_mix, norm_ffn, norm_out,
              attn_w_qkv, attn_w_o, attn_bias, sc_w_in, sc_conv_w, sc_w_out,
              ffn_w_up, ffn_conv_w, ffn_conv_b, ffn_w_down):
    bp = x_prompt.shape[0]
    conv0 = jnp.zeros((N_CONV, bp, CONV_W - 1, D_MODEL), x_prompt.dtype)
    ffn0 = jnp.zeros((DEPTH, bp, FFN_CONV_W - 1, D_FF), x_prompt.dtype)
    y_prompt, k_prompt, v_prompt, conv_prompt, ffn_prompt = run_group(
        x_prompt, c_prompt, None, None, None, conv0, ffn0,
        ada_w, ada_b, norm_mix, norm_ffn, norm_out, attn_w_qkv, attn_w_o, attn_bias,
        sc_w_in, sc_conv_w, sc_w_out, ffn_w_up, ffn_conv_w, ffn_conv_b, ffn_w_down)
    y_sample, k_sample, v_sample, conv_sample, ffn_sample = run_group(
        x_sample, c_sample, cache_k, cache_v, page_table, state_conv, state_ffn,
        ada_w, ada_b, norm_mix, norm_ffn, norm_out, attn_w_qkv, attn_w_o, attn_bias,
        sc_w_in, sc_conv_w, sc_w_out, ffn_w_up, ffn_conv_w, ffn_conv_b, ffn_w_down)
    return (y_prompt, y_sample, k_prompt, v_prompt, k_sample, v_sample,
            conv_prompt, conv_sample, ffn_prompt, ffn_sample)
```

```python
import functools
from typing import NamedTuple

import jax
import jax.numpy as jnp
from jax import lax
from jax.experimental import pallas as pl
from jax.experimental.pallas import tpu as pltpu

F32 = jnp.float32
BF16 = jnp.bfloat16

NORM_EPS = 1e-6
N_MOD = 6
HEAD_DIM = 64
LANES = 128
SUBLANES = 8
VMEM_LIMIT = 56 * 1024 * 1024

ROW_TILE = 512
FF_CHUNK = 256
ATTN_Q = 512
ATTN_S = 256
PAGES_PER_STEP = 8


class Rows(NamedTuple):
    n_groups: int
    tiles: int
    tm: int
    per_row: bool


def _params(*sem):
    return pltpu.CompilerParams(dimension_semantics=sem, vmem_limit_bytes=VMEM_LIMIT)


def _const_spec(shape, index):
    return pl.BlockSpec(shape, index, pipeline_mode=pl.Buffered(1))


def _x_spec(rows, width):
    return pl.BlockSpec((rows.tm, width), lambda g, t: (g * rows.tiles + t, 0))


def _mod_spec(rows, layer, j, d):
    if rows.per_row:
        return pl.BlockSpec((None, rows.tm, d), lambda g, t: (layer, 0, j))
    return pl.BlockSpec((None, None, 1, d), lambda g, t: (layer, g, 0, j))


def _state_in_spec(rows, width):
    if rows.per_row:
        return pl.BlockSpec((rows.tm, 2 * width), lambda g, t: (0, 0))
    return pl.BlockSpec((None, 2, width), lambda g, t: (g, 0, 0))


def _rms_mod(x, gain, scale, shift):
    y = x * lax.rsqrt(jnp.mean(x * x, axis=-1, keepdims=True) + NORM_EPS)
    return (y * gain) * (1.0 + scale) + shift


def _silu(a):
    return a / (1.0 + jnp.exp(-a))


def _ada_kernel(cp_ref, cs_ref, w_ref, b_ref, mp_ref, ms_ref):
    w = w_ref[...].astype(BF16)
    b = b_ref[...]
    for c_ref, o_ref in ((cp_ref, mp_ref), (cs_ref, ms_ref)):
        s = _silu(c_ref[...]).astype(BF16)
        o_ref[...] = jnp.dot(s, w, preferred_element_type=F32) + b


def _ada_mod(c_prompt, c_sample, ada_w, ada_b):
    depth, d, n = ada_w.shape
    tn = n // 4
    bp, bs = c_prompt.shape[0], c_sample.shape[0]
    return pl.pallas_call(
        _ada_kernel,
        out_shape=(jax.ShapeDtypeStruct((depth, bp, n), F32), jax.ShapeDtypeStruct((depth, bs, n), F32)),
        grid=(depth, n // tn),
        in_specs=[
            pl.BlockSpec((bp, d), lambda i, j: (0, 0)),
            pl.BlockSpec((bs, d), lambda i, j: (0, 0)),
            pl.BlockSpec((None, d, tn), lambda i, j: (i, 0, j)),
            pl.BlockSpec((None, 1, tn), lambda i, j: (i, 0, j)),
        ],
        out_specs=(pl.BlockSpec((None, bp, tn), lambda i, j: (i, 0, j)),
                   pl.BlockSpec((None, bs, tn), lambda i, j: (i, 0, j))),
        compiler_params=_params("arbitrary", "arbitrary"),
        name="ada_mod",
    )(c_prompt, c_sample, ada_w, ada_b.reshape(depth, 1, n))


def _qkv_kernel(x_ref, sh_ref, sc_ref, gain_ref, w_ref, q_ref, k_ref, v_ref, *, d):
    h = _rms_mod(x_ref[...], gain_ref[...], sc_ref[...], sh_ref[...]).astype(BF16)
    q = jnp.dot(h, w_ref[:, 0:d], preferred_element_type=F32)
    q_ref[...] = (q * (HEAD_DIM ** -0.5)).astype(BF16)
    k_ref[...] = jnp.dot(h, w_ref[:, d:2 * d], preferred_element_type=F32)
    v_ref[...] = jnp.dot(h, w_ref[:, 2 * d:3 * d], preferred_element_type=F32)


def _qkv(x, mod, gain, w_qkv, rows, layer):
    n, d = x.shape
    out = pl.BlockSpec((rows.tm, d), lambda g, t: (g * rows.tiles + t, 0))
    return pl.pallas_call(
        functools.partial(_qkv_kernel, d=d),
        out_shape=(jax.ShapeDtypeStruct((n, d), BF16), jax.ShapeDtypeStruct((n, d), F32),
                   jax.ShapeDtypeStruct((n, d), F32)),
        grid=(rows.n_groups, rows.tiles),
        in_specs=[
            _x_spec(rows, d),
            _mod_spec(rows, layer, 0, d),
            _mod_spec(rows, layer, 1, d),
            pl.BlockSpec((None, 1, d), lambda g, t: (layer, 0, 0)),
            _const_spec((d, 3 * d), lambda g, t: (0, 0)),
        ],
        out_specs=(out, out, out),
        compiler_params=_params("arbitrary", "arbitrary"),
        name="qkv_proj",
    )(x, mod, mod, gain, w_qkv)


def _softplus(z):
    return jnp.maximum(z, 0.0) + jnp.log(1.0 + jnp.exp(-jnp.abs(z)))


def _strict_tri(n, later_on_rows):
    r = lax.broadcasted_iota(jnp.int32, (n, n), 0)
    c = lax.broadcasted_iota(jnp.int32, (n, n), 1)
    later = (r > c) if later_on_rows else (c > r)
    return jnp.where(later, -1.0, 0.0).astype(BF16)


def _attn_kernel(bias_ref, q_ref, k_ref, v_ref, o_ref, kb_ref, vb_ref, acc_ref, run_ref, *, layer_a, tq, ts):
    hp = pl.program_id(1)
    qi = pl.program_id(2)
    nsub = tq // ts

    @pl.when(qi == 0)
    def _():
        kb_ref[...] = k_ref[...].astype(BF16)
        vb_ref[...] = v_ref[...].astype(BF16)

    lane = lax.broadcasted_iota(jnp.int32, (1, LANES), 1)
    first = lane < HEAD_DIM
    q2 = q_ref[...].astype(F32)
    q_head = ((q2 * jnp.where(first, 1.0, 0.0)).astype(BF16), (q2 * jnp.where(first, 0.0, 1.0)).astype(BF16))
    bias = (bias_ref[layer_a, 2 * hp], bias_ref[layer_a, 2 * hp + 1])
    tri = _strict_tri(ts, later_on_rows=True)
    causal = lax.broadcasted_iota(jnp.int32, (ts, ts), 1) < lax.broadcasted_iota(jnp.int32, (ts, ts), 0)

    acc_ref[...] = jnp.zeros_like(acc_ref)
    run_ref[...] = jnp.zeros_like(run_ref)

    def block(h, r0, nrows, k0, mask):
        qh = q_head[h][r0:r0 + nrows]
        ks = kb_ref[pl.ds(k0, ts), :]
        vs = vb_ref[pl.ds(k0, ts), :]
        z = lax.dot_general(qh, ks, (((1,), (1,)), ((), ())), preferred_element_type=F32) + bias[h]
        sp = _softplus(z)
        log_beta = z - sp
        if mask is not None:
            sp = jnp.where(mask, sp, 0.0)
        later = jnp.dot(sp.astype(BF16), tri, preferred_element_type=F32)
        p = jnp.exp(log_beta + later)
        if mask is not None:
            p = jnp.where(mask, p, 0.0)
        pv = jnp.dot(p.astype(BF16), vs, preferred_element_type=F32)
        run = run_ref[h, r0:r0 + nrows, :]
        acc_ref[h, r0:r0 + nrows, :] += jnp.exp(run) * pv
        run_ref[h, r0:r0 + nrows, :] = run - jnp.sum(sp, axis=-1, keepdims=True)

    base = pl.multiple_of(qi * tq, tq)
    for kc in reversed(range(nsub)):
        for h in range(2):
            block(h, kc * ts, ts, base + kc * ts, causal)
            if kc + 1 < nsub:
                block(h, (kc + 1) * ts, tq - (kc + 1) * ts, base + kc * ts, None)

    def body(it, carry):
        k_base = pl.multiple_of((qi - 1 - it) * tq, tq)
        for kc in reversed(range(nsub)):
            for h in range(2):
                block(h, 0, tq, k_base + kc * ts, None)
        return carry

    lax.fori_loop(0, qi, body, 0)
    o_ref[...] = jnp.where(first, acc_ref[0], acc_ref[1]).astype(o_ref.dtype)


def _attention(q, k, v, attn_bias, layer_a, batch, seq):
    n, d = q.shape
    tq = min(ATTN_Q, seq)
    ts = min(ATTN_S, tq)
    nq = seq // tq
    qo = pl.BlockSpec((tq, LANES), lambda b, hp, qi: (b * nq + qi, hp))
    kv = pl.BlockSpec((seq, LANES), lambda b, hp, qi: (b, hp))
    return pl.pallas_call(
        functools.partial(_attn_kernel, layer_a=layer_a, tq=tq, ts=ts),
        out_shape=jax.ShapeDtypeStruct((n, d), BF16),
        grid=(batch, d // LANES, nq),
        in_specs=[pl.BlockSpec(memory_space=pltpu.SMEM), qo, kv, kv],
        out_specs=qo,
        scratch_shapes=[pltpu.VMEM((seq, LANES), BF16), pltpu.VMEM((seq, LANES), BF16),
                        pltpu.VMEM((2, tq, LANES), F32), pltpu.VMEM((2, tq, LANES), F32)],
        compiler_params=_params("arbitrary", "arbitrary", "arbitrary"),
        name="sb_attention",
    )(attn_bias, q, k, v)


def _dec_attn_kernel(pt_ref, q_ref, bias_ref, *refs, pps, d):
    k_refs, v_refs = refs[:pps], refs[pps:2 * pps]
    o_ref, acc_ref, run_ref = refs[2 * pps:]
    c = pl.program_id(1)
    page = k_refs[0].shape[0]

    @pl.when(c == 0)
    def _():
        acc_ref[...] = jnp.zeros_like(acc_ref)
        run_ref[...] = jnp.zeros_like(run_ref)

    row = lax.broadcasted_iota(jnp.int32, (LANES, d), 0)
    col = lax.broadcasted_iota(jnp.int32, (LANES, d), 1)
    own = jnp.where((col // HEAD_DIM) == row, 1.0, 0.0)
    expand = own.astype(BF16)
    q_rows = (own * q_ref[...].astype(F32)).astype(BF16)
    tri = _strict_tri(page, later_on_rows=False)
    bias = bias_ref[...]

    for i in reversed(range(pps)):
        kb = k_refs[i][...].astype(BF16)
        z = lax.dot_general(kb, q_rows, (((1,), (1,)), ((), ())), preferred_element_type=F32) + bias
        sp = _softplus(z)
        later = jnp.dot(tri, sp.astype(BF16), preferred_element_type=F32)
        run = run_ref[...]
        p = jnp.exp((z - sp) + later + run)
        w = jnp.dot(p.astype(BF16), expand, preferred_element_type=F32)
        wv = w * v_refs[i][...]
        acc_ref[...] += jnp.sum(wv.reshape(page // SUBLANES, SUBLANES, d), axis=0)
        run_ref[...] = run - jnp.sum(sp, axis=0, keepdims=True)

    @pl.when(c == pl.num_programs(1) - 1)
    def _():
        o_ref[...] = jnp.sum(acc_ref[...], axis=0, keepdims=True).astype(o_ref.dtype)


def _decode_attention(q, cache_k, cache_v, page_table, attn_bias, layer_a):
    bs, d = q.shape
    n_attn, n_pool, page, heads, hd = cache_k.shape
    n_pages = page_table.shape[1]
    pps = min(PAGES_PER_STEP, n_pages)
    steps = n_pages // pps
    ck = cache_k.reshape(n_attn * n_pool, page, d)
    cv = cache_v.reshape(n_attn * n_pool, page, d)
    bias = jnp.pad(attn_bias[layer_a], (0, LANES - heads)).reshape(1, LANES)

    def page_spec(i):
        return pl.BlockSpec((None, page, d),
                            lambda b, c, pt: (layer_a * n_pool + pt[b, (steps - 1 - c) * pps + i], 0, 0))

    grid_spec = pltpu.PrefetchScalarGridSpec(
        num_scalar_prefetch=1,
        grid=(bs, steps),
        in_specs=[pl.BlockSpec((None, 1, d), lambda b, c, pt: (b, 0, 0)),
                  pl.BlockSpec((1, LANES), lambda b, c, pt: (0, 0))]
                 + [page_spec(i) for i in range(pps)] + [page_spec(i) for i in range(pps)],
        out_specs=pl.BlockSpec((None, 1, d), lambda b, c, pt: (b, 0, 0)),
        scratch_shapes=[pltpu.VMEM((SUBLANES, d), F32), pltpu.VMEM((1, LANES), F32)],
    )
    out = pl.pallas_call(
        functools.partial(_dec_attn_kernel, pps=pps, d=d),
        out_shape=jax.ShapeDtypeStruct((bs, 1, d), BF16),
        grid_spec=grid_spec,
        compiler_params=_params("arbitrary", "arbitrary"),
        name="sb_decode_attention",
    )(page_table, q.reshape(bs, 1, d), bias, *([ck] * pps), *([cv] * pps))
    return out.reshape(bs, d)


def _conv_carry_in(rows, t, prev_ref, stage_ref):
    if rows.per_row:
        return

    @pl.when(t == 0)
    def _():
        stage_ref[SUBLANES - 2:SUBLANES, :] = prev_ref[...]

    @pl.when(t != 0)
    def _():
        stage_ref[0:SUBLANES, :] = stage_ref[rows.tm:rows.tm + SUBLANES, :]


def _conv3(u, cols, conv_w_ref, rows, prev_ref, stage_ref, state_ref, width):
    tm = rows.tm
    w = conv_w_ref[:, cols]
    if rows.per_row:
        lo = slice(cols.start, cols.stop)
        hi = slice(width + cols.start, width + cols.stop)
        u2, u1 = prev_ref[:, lo], prev_ref[:, hi]
        state_ref[:, lo] = u1
        state_ref[:, hi] = u
    else:
        stage_ref[SUBLANES:SUBLANES + tm, cols] = u
        u1 = stage_ref[SUBLANES - 1:SUBLANES - 1 + tm, cols]
        u2 = stage_ref[SUBLANES - 2:SUBLANES - 2 + tm, cols]
        state_ref[:, cols] = u[tm - 2:tm]
    return w[0:1] * u2 + w[1:2] * u1 + w[2:3] * u


def _ffn_kernel(*refs, rows, d, f, has_proj, final_norm):
    it = iter(refs)
    x_ref = next(it)
    if has_proj:
        a_ref, wp_ref, g1_ref = next(it), next(it), next(it)
    sh_ref, sc_ref, g2_ref, gain_ref = next(it), next(it), next(it), next(it)
    wu_ref, cw_ref, cb_ref, wd_ref, prev_ref = next(it), next(it), next(it), next(it), next(it)
    if final_norm:
        gout_ref = next(it)
    y_ref, state_ref = next(it), next(it)
    act_ref = next(it)
    stage_ref = None if rows.per_row else next(it)
    t = pl.program_id(1)

    x = x_ref[...]
    if has_proj:
        x = x + g1_ref[...] * jnp.dot(a_ref[...], wp_ref[...], preferred_element_type=F32)
    h = _rms_mod(x, gain_ref[...], sc_ref[...], sh_ref[...]).astype(BF16)
    _conv_carry_in(rows, t, prev_ref, stage_ref)
    for c0 in range(0, f, FF_CHUNK):
        cols = slice(c0, c0 + FF_CHUNK)
        g = jnp.dot(h, wu_ref[:, cols], preferred_element_type=F32)
        u = jnp.dot(h, wu_ref[:, f + c0:f + c0 + FF_CHUNK], preferred_element_type=F32)
        a = _conv3(g, cols, cw_ref, rows, prev_ref, stage_ref, state_ref, f) + cb_ref[:, cols]
        act_ref[:, cols] = (_silu(a) * u).astype(BF16)
    x = x + g2_ref[...] * jnp.dot(act_ref[...], wd_ref[...], preferred_element_type=F32)
    if final_norm:
        x = x * lax.rsqrt(jnp.mean(x * x, axis=-1, keepdims=True) + NORM_EPS) * gout_ref[...]
    y_ref[...] = x


def _ffn(x, mod, gain, w_up, conv_w, conv_b, w_down, prev, rows, layer, proj=None, out_gain=None):
    n, d = x.shape
    f = w_down.shape[0]
    const = lambda g, t: (0, 0)
    args, specs = [x], [_x_spec(rows, d)]
    if proj is not None:
        a, w_proj = proj
        args += [a, w_proj, mod]
        specs += [_x_spec(rows, d), _const_spec((d, d), const), _mod_spec(rows, layer, 2, d)]
    args += [mod, mod, mod, gain, w_up, conv_w, conv_b.reshape(1, f), w_down, prev]
    specs += [_mod_spec(rows, layer, 3, d), _mod_spec(rows, layer, 4, d), _mod_spec(rows, layer, 5, d),
              pl.BlockSpec((None, 1, d), lambda g, t: (layer, 0, 0)),
              _const_spec((d, 2 * f), const), _const_spec((3, f), const), _const_spec((1, f), const),
              _const_spec((f, d), const), _state_in_spec(rows, f)]
    if out_gain is not None:
        args.append(out_gain.reshape(1, d))
        specs.append(_const_spec((1, d), const))
    scratch = [pltpu.VMEM((rows.tm, f), BF16)]
    if not rows.per_row:
        scratch.append(pltpu.VMEM((rows.tm + SUBLANES, f), F32))
    state_shape = (n, 2 * f) if rows.per_row else (rows.n_groups, 2, f)
    return pl.pallas_call(
        functools.partial(_ffn_kernel, rows=rows, d=d, f=f, has_proj=proj is not None,
                          final_norm=out_gain is not None),
        out_shape=(jax.ShapeDtypeStruct((n, d), F32), jax.ShapeDtypeStruct(state_shape, F32)),
        grid=(rows.n_groups, rows.tiles),
        in_specs=specs,
        out_specs=(_x_spec(rows, d), _state_in_spec(rows, f)),
        scratch_shapes=scratch,
        compiler_params=_params("arbitrary", "arbitrary"),
        name="conv_ffn",
    )(*args)


def _sconv_kernel(*refs, rows, d):
    x_ref, sh_ref, sc_ref, g1_ref, gain_ref, wi_ref, cw_ref, wo_ref, prev_ref, y_ref, state_ref, gated_ref = refs[:12]
    stage_ref = None if rows.per_row else refs[12]
    t = pl.program_id(1)
    x = x_ref[...]
    h = _rms_mod(x, gain_ref[...], sc_ref[...], sh_ref[...]).astype(BF16)
    _conv_carry_in(rows, t, prev_ref, stage_ref)
    for c0 in range(0, d, FF_CHUNK):
        cols = slice(c0, c0 + FF_CHUNK)
        b_gate = jnp.dot(h, wi_ref[:, cols], preferred_element_type=F32)
        c_gate = jnp.dot(h, wi_ref[:, d + c0:d + c0 + FF_CHUNK], preferred_element_type=F32)
        u = jnp.dot(h, wi_ref[:, 2 * d + c0:2 * d + c0 + FF_CHUNK], preferred_element_type=F32)
        y = _conv3(c_gate * u, cols, cw_ref, rows, prev_ref, stage_ref, state_ref, d)
        gated_ref[:, cols] = (b_gate * y).astype(BF16)
    y_ref[...] = x + g1_ref[...] * jnp.dot(gated_ref[...], wo_ref[...], preferred_element_type=F32)


def _short_conv(x, mod, gain, w_in, conv_w, w_out, prev, rows, layer):
    n, d = x.shape
    const = lambda g, t: (0, 0)
    scratch = [pltpu.VMEM((rows.tm, d), BF16)]
    if not rows.per_row:
        scratch.append(pltpu.VMEM((rows.tm + SUBLANES, d), F32))
    state_shape = (n, 2 * d) if rows.per_row else (rows.n_groups, 2, d)
    return pl.pallas_call(
        functools.partial(_sconv_kernel, rows=rows, d=d),
        out_shape=(jax.ShapeDtypeStruct((n, d), F32), jax.ShapeDtypeStruct(state_shape, F32)),
        grid=(rows.n_groups, rows.tiles),
        in_specs=[_x_spec(rows, d), _mod_spec(rows, layer, 0, d), _mod_spec(rows, layer, 1, d),
                  _mod_spec(rows, layer, 2, d), pl.BlockSpec((None, 1, d), lambda g, t: (layer, 0, 0)),
                  _const_spec((d, 3 * d), const), _const_spec((3, d), const), _const_spec((d, d), const),
                  _state_in_spec(rows, d)],
        out_specs=(_x_spec(rows, d), _state_in_spec(rows, d)),
        scratch_shapes=scratch,
        compiler_params=_params("arbitrary", "arbitrary"),
        name="short_conv",
    )(x, mod, mod, mod, gain, w_in, conv_w, w_out, prev)


def _run_group(x, mod, rows, decode, w, cache=None):
    depth = w["norm_mix"].shape[0]
    n_mix = 2
    new_k, new_v, new_conv, new_ffn = [], [], [], []
    for i in range(depth):
        last = i == depth - 1
        proj = None
        if i % n_mix == 0:
            a = i // n_mix
            q, k, v = _qkv(x, mod, w["norm_mix"], w["attn_w_qkv"][a], rows, i)
            if decode:
                o = _decode_attention(q, cache["k"], cache["v"], cache["page_table"], w["attn_bias"], a)
            else:
                o = _attention(q, k, v, w["attn_bias"], a, rows.n_groups, rows.tiles * rows.tm)
            new_k.append(k)
            new_v.append(v)
            proj = (o, w["attn_w_o"][a])
        else:
            j = i // n_mix
            x, st = _short_conv(x, mod, w["norm_mix"], w["sc_w_in"][j], w["sc_conv_w"][j], w["sc_w_out"][j],
                                w["conv_prev"][j], rows, i)
            new_conv.append(st)
        x, st = _ffn(x, mod, w["norm_ffn"], w["ffn_w_up"][i], w["ffn_conv_w"][i], w["ffn_conv_b"][i],
                     w["ffn_w_down"][i], w["ffn_prev"][i], rows, i, proj=proj,
                     out_gain=w["norm_out"] if last else None)
        new_ffn.append(st)
    return x, new_k, new_v, new_conv, new_ffn


def kernel(x_prompt, x_sample, cache_k, cache_v, state_conv, state_ffn, page_table, c_prompt, c_sample, ada_w, ada_b, norm_mix, norm_ffn, norm_out, attn_w_qkv, attn_w_o, attn_bias, sc_w_in, sc_conv_w, sc_w_out, ffn_w_up, ffn_conv_w, ffn_conv_b, ffn_w_down):
    bp, seq, d = x_prompt.shape
    bs, dec_seq, _ = x_sample.shape
    assert dec_seq == 1, "the sample group decodes one token per sequence"
    depth = ada_w.shape[0]
    f = ffn_w_down.shape[1]
    heads = d // HEAD_DIM
    n_conv = state_conv.shape[0]

    mod_p, mod_s = _ada_mod(c_prompt, c_sample, ada_w, ada_b)
    mod_p = mod_p.reshape(depth, bp, 1, N_MOD * d)

    shared = dict(
        norm_mix=norm_mix.reshape(depth, 1, d), norm_ffn=norm_ffn.reshape(depth, 1, d), norm_out=norm_out,
        attn_w_qkv=attn_w_qkv.astype(BF16), attn_w_o=attn_w_o.astype(BF16), attn_bias=attn_bias,
        sc_w_in=sc_w_in.astype(BF16), sc_conv_w=sc_conv_w, sc_w_out=sc_w_out.astype(BF16),
        ffn_w_up=ffn_w_up.astype(BF16), ffn_conv_w=ffn_conv_w, ffn_conv_b=ffn_conv_b,
        ffn_w_down=ffn_w_down.astype(BF16))

    tm = min(ROW_TILE, seq)
    rows_p = Rows(n_groups=bp, tiles=seq // tm, tm=tm, per_row=False)
    w_p = dict(shared, conv_prev=jnp.zeros((n_conv, bp, 2, d), F32), ffn_prev=jnp.zeros((depth, bp, 2, f), F32))
    y_p, k_p, v_p, conv_p, ffn_p = _run_group(x_prompt.reshape(bp * seq, d), mod_p, rows_p, False, w_p)

    rows_s = Rows(n_groups=1, tiles=1, tm=bs, per_row=True)
    w_s = dict(shared, conv_prev=state_conv.reshape(n_conv, bs, 2 * d), ffn_prev=state_ffn.reshape(depth, bs, 2 * f))
    cache = dict(k=cache_k, v=cache_v, page_table=page_table)
    y_s, k_s, v_s, conv_s, ffn_s = _run_group(x_sample.reshape(bs, d), mod_s, rows_s, True, w_s, cache)

    def heads_of(ts, b, t):
        return jnp.stack(ts).reshape(len(ts), b, t, heads, HEAD_DIM)

    return (y_p.reshape(bp, seq, d), y_s.reshape(bs, 1, d),
            heads_of(k_p, bp, seq), heads_of(v_p, bp, seq), heads_of(k_s, bs, 1), heads_of(v_s, bs, 1),
            jnp.stack(conv_p), jnp.stack(conv_s).reshape(n_conv, bs, 2, d),
            jnp.stack(ffn_p), jnp.stack(ffn_s).reshape(depth, bs, 2, f))
```

```python
import functools
from typing import NamedTuple

import jax
import jax.numpy as jnp
from jax import lax
from jax.experimental import pallas as pl
from jax.experimental.pallas import tpu as pltpu

F32 = jnp.float32
BF16 = jnp.bfloat16

NORM_EPS = 1e-6
N_MOD = 6
HEAD_DIM = 64
LANES = 128
SUBLANES = 8
VMEM_LIMIT = 56 * 1024 * 1024

ROW_TILE = 512
FF_CHUNK = 256
ATTN_Q = 512
ATTN_S = 256
PAGES_PER_STEP = 8


class Rows(NamedTuple):
    n_groups: int
    tiles: int
    tm: int
    per_row: bool


def _params(*sem):
    return pltpu.CompilerParams(dimension_semantics=sem, vmem_limit_bytes=VMEM_LIMIT)


def _const_spec(shape, index):
    return pl.BlockSpec(shape, index, pipeline_mode=pl.Buffered(1))


def _x_spec(rows, width):
    return pl.BlockSpec((rows.tm, width), lambda g, t: (g * rows.tiles + t, 0))


def _mod_spec(rows, layer, j, d):
    if rows.per_row:
        return pl.BlockSpec((None, rows.tm, d), lambda g, t: (layer, 0, j))
    return pl.BlockSpec((None, None, 1, d), lambda g, t: (layer, g, 0, j))


def _state_in_spec(rows, width):
    if rows.per_row:
        return pl.BlockSpec((rows.tm, 2 * width), lambda g, t: (0, 0))
    return pl.BlockSpec((None, 2, width), lambda g, t: (g, 0, 0))


def _rms_mod(x, gain, scale, shift):
    y = x * lax.rsqrt(jnp.mean(x * x, axis=-1, keepdims=True) + NORM_EPS)
    return (y * gain) * (1.0 + scale) + shift


def _silu(a):
    return a / (1.0 + jnp.exp(-a))


def _ada_kernel(cp_ref, cs_ref, w_ref, b_ref, mp_ref, ms_ref):
    w = w_ref[...].astype(BF16)
    b = b_ref[...]
    for c_ref, o_ref in ((cp_ref, mp_ref), (cs_ref, ms_ref)):
        s = _silu(c_ref[...]).astype(BF16)
        o_ref[...] = jnp.dot(s, w, preferred_element_type=F32) + b


def _ada_mod(c_prompt, c_sample, ada_w, ada_b):
    depth, d, n = ada_w.shape
    tn = n // 4
    bp, bs = c_prompt.shape[0], c_sample.shape[0]
    return pl.pallas_call(
        _ada_kernel,
        out_shape=(jax.ShapeDtypeStruct((depth, bp, n), F32), jax.ShapeDtypeStruct((depth, bs, n), F32)),
        grid=(depth, n // tn),
        in_specs=[
            pl.BlockSpec((bp, d), lambda i, j: (0, 0)),
            pl.BlockSpec((bs, d), lambda i, j: (0, 0)),
            pl.BlockSpec((None, d, tn), lambda i, j: (i, 0, j)),
            pl.BlockSpec((None, 1, tn), lambda i, j: (i, 0, j)),
        ],
        out_specs=(pl.BlockSpec((None, bp, tn), lambda i, j: (i, 0, j)),
                   pl.BlockSpec((None, bs, tn), lambda i, j: (i, 0, j))),
        compiler_params=_params("arbitrary", "arbitrary"),
        name="ada_mod",
    )(c_prompt, c_sample, ada_w, ada_b.reshape(depth, 1, n))


def _qkv_kernel(x_ref, sh_ref, sc_ref, gain_ref, wq_ref, wkv_ref, q_ref, k_ref, v_ref, *, d, q_scale, time_on_lanes):
    h = _rms_mod(x_ref[...], gain_ref[...], sc_ref[...], sh_ref[...]).astype(BF16)
    q = jnp.dot(h, wq_ref[...], preferred_element_type=F32)
    q_ref[...] = (q * q_scale).astype(q_ref.dtype)
    if time_on_lanes:
        nt = (((1,), (1,)), ((), ()))
        k_ref[...] = lax.dot_general(wkv_ref[0:d, :], h, nt, preferred_element_type=F32)
        v_ref[...] = lax.dot_general(wkv_ref[d:2 * d, :], h, nt, preferred_element_type=F32)
    else:
        k_ref[...] = jnp.dot(h, wkv_ref[:, 0:d], preferred_element_type=F32)
        v_ref[...] = jnp.dot(h, wkv_ref[:, d:2 * d], preferred_element_type=F32)


def _qkv(x, mod, gain, w_q, w_kv, rows, layer, q_scale, q_dtype, time_on_lanes):
    n, d = x.shape
    row_out = pl.BlockSpec((rows.tm, d), lambda g, t: (g * rows.tiles + t, 0))
    if time_on_lanes:
        kv_shape = jax.ShapeDtypeStruct((rows.n_groups, d, rows.tiles * rows.tm), F32)
        kv_out = pl.BlockSpec((None, d, rows.tm), lambda g, t: (g, 0, t))
    else:
        kv_shape = jax.ShapeDtypeStruct((n, d), F32)
        kv_out = row_out
    return pl.pallas_call(
        functools.partial(_qkv_kernel, d=d, q_scale=q_scale, time_on_lanes=time_on_lanes),
        out_shape=(jax.ShapeDtypeStruct((n, d), q_dtype), kv_shape, kv_shape),
        grid=(rows.n_groups, rows.tiles),
        in_specs=[
            _x_spec(rows, d),
            _mod_spec(rows, layer, 0, d),
            _mod_spec(rows, layer, 1, d),
            pl.BlockSpec((None, 1, d), lambda g, t: (layer, 0, 0)),
            _const_spec((d, d), lambda g, t: (0, 0)),
            _const_spec(w_kv.shape, lambda g, t: (0, 0)),
        ],
        out_specs=(row_out, kv_out, kv_out),
        compiler_params=_params("arbitrary", "arbitrary"),
        name="qkv_proj",
    )(x, mod, mod, gain, w_q, w_kv)


LOG2E = 1.4426950408889634


def _softplus(z):
    return jnp.maximum(z, 0.0) + jnp.log(1.0 + jnp.exp(-jnp.abs(z)))


def _softplus2(z2):
    return jnp.maximum(z2, 0.0) + jnp.log(1.0 + jnp.exp2(-jnp.abs(z2))) * LOG2E


def _split_bf16(a):
    hi = a.astype(BF16)
    return hi, (a - hi.astype(F32)).astype(BF16)


def _strict_tri(n, later_on_rows):
    r = lax.broadcasted_iota(jnp.int32, (n, n), 0)
    c = lax.broadcasted_iota(jnp.int32, (n, n), 1)
    later = (r > c) if later_on_rows else (c > r)
    return jnp.where(later, -1.0, 0.0).astype(BF16)


def _attn_kernel(bias_ref, q_ref, kt_ref, vt_ref, o_ref, kb_ref, vb_ref, acc_ref, run_ref, *, layer_a, tq, ts, seq):
    hp = pl.program_id(1)
    qi = pl.program_id(2)
    nsub = tq // ts

    own_lo = (0, HEAD_DIM)
    one_lo = (HEAD_DIM, 0)

    @pl.when(qi == 0)
    def _():
        row = lax.broadcasted_iota(jnp.int32, (LANES, 1), 0)
        for h in range(2):
            own = (row >= own_lo[h]) & (row < own_lo[h] + HEAD_DIM)
            ones = jnp.where((row >= one_lo[h]) & (row < one_lo[h] + 2), 1.0, 0.0)
            for j in range(seq // ts):
                kb_ref[h, j] = jnp.where(own, kt_ref[:, j * ts:(j + 1) * ts], ones).astype(BF16)
        for j in range(seq // tq):
            vb_ref[j] = vt_ref[:, j * tq:(j + 1) * tq].astype(BF16)

    lane = lax.broadcasted_iota(jnp.int32, (1, LANES), 1)
    first = lane < HEAD_DIM
    q2 = q_ref[...].astype(F32)
    q_head = []
    for h in range(2):
        own = jnp.where((lane >= own_lo[h]) & (lane < own_lo[h] + HEAD_DIM), 1.0, 0.0)
        b = jnp.full((1, LANES), bias_ref[layer_a, 2 * hp + h] * LOG2E, F32)
        b_hi = b.astype(BF16).astype(F32)
        b_lanes = jnp.where(lane == one_lo[h], b_hi, jnp.where(lane == one_lo[h] + 1, b - b_hi, 0.0))
        q_head.append((q2 * own + b_lanes).astype(BF16))
    tri = _strict_tri(ts, later_on_rows=True)
    causal = lax.broadcasted_iota(jnp.int32, (ts, ts), 1) < lax.broadcasted_iota(jnp.int32, (ts, ts), 0)

    acc_ref[...] = jnp.zeros_like(acc_ref)
    run_ref[...] = jnp.zeros_like(run_ref)

    def sweep(h, r0, nrows, tile, subs):
        qh = q_head[h][r0:r0 + nrows]
        swept = None
        probs = []
        for kc, mask in subs:
            z = jnp.dot(qh, kb_ref[h, tile * nsub + kc], preferred_element_type=F32)
            sp = _softplus2(z)
            log_beta = z - sp
            if mask is not None:
                sp = jnp.where(mask, sp, 0.0)
            later = jnp.dot(sp.astype(BF16), tri, preferred_element_type=F32)
            x = log_beta + later
            if swept is not None:
                x = x + swept
            p = jnp.exp2(x)
            if mask is not None:
                p = jnp.where(mask, p, 0.0)
            probs.append(p.astype(BF16))
            total = later[:, 0:1] - sp[:, 0:1]
            swept = total if swept is None else swept + total
        lo, hi = subs[-1][0], subs[0][0] + 1
        p_all = probs[0] if len(probs) == 1 else jnp.concatenate(probs[::-1], axis=1)
        vs = vb_ref[tile, :, lo * ts:hi * ts]
        pv = lax.dot_general(p_all, vs, (((1,), (1,)), ((), ())), preferred_element_type=F32)
        run = run_ref[h, r0:r0 + nrows, :]
        acc_ref[h, r0:r0 + nrows, :] += jnp.exp2(run) * pv
        run_ref[h, r0:r0 + nrows, :] = run + swept

    for rc in range(nsub):
        subs = [(rc, causal)] + [(kc, None) for kc in reversed(range(rc))]
        for h in range(2):
            sweep(h, rc * ts, ts, qi, subs)

    full = [(kc, None) for kc in reversed(range(nsub))]

    def body(it, carry):
        for tile in (qi - 1 - 2 * it, qi - 2 - 2 * it):
            for h in range(2):
                sweep(h, 0, tq, tile, full)
        return carry

    lax.fori_loop(0, lax.shift_right_logical(qi, 1), body, 0)

    @pl.when((qi & 1) == 1)
    def _():
        for h in range(2):
            sweep(h, 0, tq, 0, full)

    o_ref[...] = jnp.where(first, acc_ref[0], acc_ref[1]).astype(o_ref.dtype)


def _attention(q, kt, vt, attn_bias, layer_a):
    n, d = q.shape
    batch, _, seq = kt.shape
    tq = min(ATTN_Q, seq)
    ts = min(ATTN_S, tq)
    nq = seq // tq
    qo = pl.BlockSpec((tq, LANES), lambda b, hp, qi: (b * nq + qi, hp))
    kv = pl.BlockSpec((None, LANES, seq), lambda b, hp, qi: (b, hp, 0))
    return pl.pallas_call(
        functools.partial(_attn_kernel, layer_a=layer_a, tq=tq, ts=ts, seq=seq),
        out_shape=jax.ShapeDtypeStruct((n, d), BF16),
        grid=(batch, d // LANES, nq),
        in_specs=[pl.BlockSpec(memory_space=pltpu.SMEM), qo, kv, kv],
        out_specs=qo,
        scratch_shapes=[pltpu.VMEM((2, seq // ts, LANES, ts), BF16), pltpu.VMEM((seq // tq, LANES, tq), BF16),
                        pltpu.VMEM((2, tq, LANES), F32), pltpu.VMEM((2, tq, LANES), F32)],
        compiler_params=_params("arbitrary", "arbitrary", "arbitrary"),
        name="sb_attention",
    )(attn_bias, q, kt, vt)


def _dec_attn_kernel(pt_ref, q_ref, bias_ref, *refs, pps, heads):
    k_refs, v_refs = refs[:pps], refs[pps:2 * pps]
    o_ref, qcol_ref, z_ref, p_ref, acc_ref, run_ref = refs[2 * pps:]
    c = pl.program_id(1)
    d, page = k_refs[0].shape
    hd = d // heads
    nrow = heads * pps

    @pl.when(c == 0)
    def _():
        qcol_ref[...] = jnp.broadcast_to(q_ref[...], (page, d)).T
        acc_ref[...] = jnp.zeros_like(acc_ref)
        run_ref[...] = jnp.zeros_like(run_ref)

    for h in range(heads):
        hs = slice(h * hd, (h + 1) * hd)
        qh = qcol_ref[hs, :]
        for i in range(pps):
            r = h * pps + i
            z_ref[r:r + 1, :] = jnp.sum(k_refs[i][hs, :] * qh, axis=0, keepdims=True)

    z = z_ref[...] + bias_ref[...]
    sp = _softplus(z)
    r_idx = lax.broadcasted_iota(jnp.int32, (nrow, nrow), 0)
    c_idx = lax.broadcasted_iota(jnp.int32, (nrow, nrow), 1)
    same_head = (r_idx // pps) == (c_idx // pps)
    later_pages = jnp.where(same_head, jnp.where(c_idx > r_idx, -1.0, 0.0), 0.0).astype(BF16)
    all_pages = jnp.where(same_head, -1.0, 0.0).astype(BF16)
    tri = _strict_tri(page, later_on_rows=True)

    sp_hi, sp_lo = _split_bf16(sp)
    later = jnp.dot(sp_hi, tri, preferred_element_type=F32) + jnp.dot(sp_lo, tri, preferred_element_type=F32)
    tot_hi, tot_lo = _split_bf16(jnp.broadcast_to(jnp.sum(sp, axis=-1, keepdims=True), (nrow, page)))
    later = later + (jnp.dot(later_pages, tot_hi, preferred_element_type=F32)
                     + jnp.dot(later_pages, tot_lo, preferred_element_type=F32))
    run = run_ref[...]
    p_ref[...] = jnp.exp((z - sp) + later + run)
    run_ref[...] = run + (jnp.dot(all_pages, tot_hi, preferred_element_type=F32)
                          + jnp.dot(all_pages, tot_lo, preferred_element_type=F32))

    for h in range(heads):
        hs = slice(h * hd, (h + 1) * hd)
        part = None
        for i in range(pps):
            r = h * pps + i
            term = p_ref[r:r + 1, :] * v_refs[i][hs, :]
            part = term if part is None else part + term
        acc_ref[hs, :] += part

    @pl.when(c == pl.num_programs(1) - 1)
    def _():
        o_ref[...] = jnp.sum(acc_ref[...].T, axis=0, keepdims=True).astype(o_ref.dtype)


def _decode_attention(q, cache_k, cache_v, page_table, attn_bias, layer_a):
    bs, d = q.shape
    n_attn, n_pool, page, heads, hd = cache_k.shape
    assert page == LANES, "one cache page must fill the lane axis"
    n_pages = page_table.shape[1]
    pps = min(PAGES_PER_STEP, n_pages)
    steps = n_pages // pps
    nrow = heads * pps
    ck = jnp.transpose(cache_k, (0, 1, 3, 4, 2)).reshape(n_attn * n_pool, d, page)
    cv = jnp.transpose(cache_v, (0, 1, 3, 4, 2)).reshape(n_attn * n_pool, d, page)
    bias = jnp.broadcast_to(attn_bias[layer_a][:, None, None], (heads, pps, page)).reshape(nrow, page)

    def page_spec(i):
        return pl.BlockSpec((None, d, page),
                            lambda b, c, pt: (layer_a * n_pool + pt[b, (steps - 1 - c) * pps + i], 0, 0))

    grid_spec = pltpu.PrefetchScalarGridSpec(
        num_scalar_prefetch=1,
        grid=(bs, steps),
        in_specs=[pl.BlockSpec((None, 1, d), lambda b, c, pt: (b, 0, 0)),
                  pl.BlockSpec((nrow, page), lambda b, c, pt: (0, 0))]
                 + [page_spec(i) for i in range(pps)] + [page_spec(i) for i in range(pps)],
        out_specs=pl.BlockSpec((None, 1, d), lambda b, c, pt: (b, 0, 0)),
        scratch_shapes=[pltpu.VMEM((d, page), F32), pltpu.VMEM((nrow, page), F32), pltpu.VMEM((nrow, page), F32),
                        pltpu.VMEM((d, page), F32), pltpu.VMEM((nrow, page), F32)],
    )
    out = pl.pallas_call(
        functools.partial(_dec_attn_kernel, pps=pps, heads=heads),
        out_shape=jax.ShapeDtypeStruct((bs, 1, d), BF16),
        grid_spec=grid_spec,
        compiler_params=_params("arbitrary", "arbitrary"),
        name="sb_decode_attention",
    )(page_table, q.reshape(bs, 1, d), bias, *([ck] * pps), *([cv] * pps))
    return out.reshape(bs, d)


def _conv_carry_in(rows, t, prev_ref, stage_ref):
    if rows.per_row:
        return

    @pl.when(t == 0)
    def _():
        stage_ref[SUBLANES - 2:SUBLANES, :] = prev_ref[...]

    @pl.when(t != 0)
    def _():
        stage_ref[0:SUBLANES, :] = stage_ref[rows.tm:rows.tm + SUBLANES, :]


def _conv3(u, cols, conv_w_ref, rows, prev_ref, stage_ref, state_ref, width):
    tm = rows.tm
    w = conv_w_ref[:, cols]
    if rows.per_row:
        lo = slice(cols.start, cols.stop)
        hi = slice(width + cols.start, width + cols.stop)
        u2, u1 = prev_ref[:, lo], prev_ref[:, hi]
        state_ref[:, lo] = u1
        state_ref[:, hi] = u
    else:
        stage_ref[SUBLANES:SUBLANES + tm, cols] = u
        u1 = stage_ref[SUBLANES - 1:SUBLANES - 1 + tm, cols]
        u2 = stage_ref[SUBLANES - 2:SUBLANES - 2 + tm, cols]
        state_ref[:, cols] = u[tm - 2:tm]
    return w[0:1] * u2 + w[1:2] * u1 + w[2:3] * u


def _ffn_kernel(*refs, rows, d, f, has_proj, final_norm):
    it = iter(refs)
    x_ref = next(it)
    if has_proj:
        a_ref, wp_ref, g1_ref = next(it), next(it), next(it)
    sh_ref, sc_ref, g2_ref, gain_ref = next(it), next(it), next(it), next(it)
    wu_ref, cw_ref, cb_ref, wd_ref, prev_ref = next(it), next(it), next(it), next(it), next(it)
    if final_norm:
        gout_ref = next(it)
    y_ref, state_ref = next(it), next(it)
    act_ref = next(it)
    stage_ref = None if rows.per_row else next(it)
    t = pl.program_id(1)

    x = x_ref[...]
    if has_proj:
        x = x + g1_ref[...] * jnp.dot(a_ref[...], wp_ref[...], preferred_element_type=F32)
    h = _rms_mod(x, gain_ref[...], sc_ref[...], sh_ref[...]).astype(BF16)
    _conv_carry_in(rows, t, prev_ref, stage_ref)
    for c0 in range(0, f, FF_CHUNK):
        cols = slice(c0, c0 + FF_CHUNK)
        g = jnp.dot(h, wu_ref[:, cols], preferred_element_type=F32)
        u = jnp.dot(h, wu_ref[:, f + c0:f + c0 + FF_CHUNK], preferred_element_type=F32)
        a = _conv3(g, cols, cw_ref, rows, prev_ref, stage_ref, state_ref, f) + cb_ref[:, cols]
        act_ref[:, cols] = (_silu(a) * u).astype(BF16)
    x = x + g2_ref[...] * jnp.dot(act_ref[...], wd_ref[...], preferred_element_type=F32)
    if final_norm:
        x = x * lax.rsqrt(jnp.mean(x * x, axis=-1, keepdims=True) + NORM_EPS) * gout_ref[...]
    y_ref[...] = x


def _ffn(x, mod, gain, w_up, conv_w, conv_b, w_down, prev, rows, layer, proj=None, out_gain=None):
    n, d = x.shape
    f = w_down.shape[0]
    const = lambda g, t: (0, 0)
    args, specs = [x], [_x_spec(rows, d)]
    if proj is not None:
        a, w_proj = proj
        args += [a, w_proj, mod]
        specs += [_x_spec(rows, d), _const_spec((d, d), const), _mod_spec(rows, layer, 2, d)]
    args += [mod, mod, mod, gain, w_up, conv_w, conv_b.reshape(1, f), w_down, prev]
    specs += [_mod_spec(rows, layer, 3, d), _mod_spec(rows, layer, 4, d), _mod_spec(rows, layer, 5, d),
              pl.BlockSpec((None, 1, d), lambda g, t: (layer, 0, 0)),
              _const_spec((d, 2 * f), const), _const_spec((3, f), const), _const_spec((1, f), const),
              _const_spec((f, d), const), _state_in_spec(rows, f)]
    if out_gain is not None:
        args.append(out_gain.reshape(1, d))
        specs.append(_const_spec((1, d), const))
    scratch = [pltpu.VMEM((rows.tm, f), BF16)]
    if not rows.per_row:
        scratch.append(pltpu.VMEM((rows.tm + SUBLANES, f), F32))
    state_shape = (n, 2 * f) if rows.per_row else (rows.n_groups, 2, f)
    return pl.pallas_call(
        functools.partial(_ffn_kernel, rows=rows, d=d, f=f, has_proj=proj is not None,
                          final_norm=out_gain is not None),
        out_shape=(jax.ShapeDtypeStruct((n, d), F32), jax.ShapeDtypeStruct(state_shape, F32)),
        grid=(rows.n_groups, rows.tiles),
        in_specs=specs,
        out_specs=(_x_spec(rows, d), _state_in_spec(rows, f)),
        scratch_shapes=scratch,
        compiler_params=_params("arbitrary", "arbitrary"),
        name="conv_ffn",
    )(*args)


def _sconv_kernel(*refs, rows, d):
    x_ref, sh_ref, sc_ref, g1_ref, gain_ref, wi_ref, cw_ref, wo_ref, prev_ref, y_ref, state_ref, gated_ref = refs[:12]
    stage_ref = None if rows.per_row else refs[12]
    t = pl.program_id(1)
    x = x_ref[...]
    h = _rms_mod(x, gain_ref[...], sc_ref[...], sh_ref[...]).astype(BF16)
    _conv_carry_in(rows, t, prev_ref, stage_ref)
    for c0 in range(0, d, FF_CHUNK):
        cols = slice(c0, c0 + FF_CHUNK)
        b_gate = jnp.dot(h, wi_ref[:, cols], preferred_element_type=F32)
        c_gate = jnp.dot(h, wi_ref[:, d + c0:d + c0 + FF_CHUNK], preferred_element_type=F32)
        u = jnp.dot(h, wi_ref[:, 2 * d + c0:2 * d + c0 + FF_CHUNK], preferred_element_type=F32)
        y = _conv3(c_gate * u, cols, cw_ref, rows, prev_ref, stage_ref, state_ref, d)
        gated_ref[:, cols] = (b_gate * y).astype(BF16)
    y_ref[...] = x + g1_ref[...] * jnp.dot(gated_ref[...], wo_ref[...], preferred_element_type=F32)


def _short_conv(x, mod, gain, w_in, conv_w, w_out, prev, rows, layer):
    n, d = x.shape
    const = lambda g, t: (0, 0)
    scratch = [pltpu.VMEM((rows.tm, d), BF16)]
    if not rows.per_row:
        scratch.append(pltpu.VMEM((rows.tm + SUBLANES, d), F32))
    state_shape = (n, 2 * d) if rows.per_row else (rows.n_groups, 2, d)
    return pl.pallas_call(
        functools.partial(_sconv_kernel, rows=rows, d=d),
        out_shape=(jax.ShapeDtypeStruct((n, d), F32), jax.ShapeDtypeStruct(state_shape, F32)),
        grid=(rows.n_groups, rows.tiles),
        in_specs=[_x_spec(rows, d), _mod_spec(rows, layer, 0, d), _mod_spec(rows, layer, 1, d),
                  _mod_spec(rows, layer, 2, d), pl.BlockSpec((None, 1, d), lambda g, t: (layer, 0, 0)),
                  _const_spec((d, 3 * d), const), _const_spec((3, d), const), _const_spec((d, d), const),
                  _state_in_spec(rows, d)],
        out_specs=(_x_spec(rows, d), _state_in_spec(rows, d)),
        scratch_shapes=scratch,
        compiler_params=_params("arbitrary", "arbitrary"),
        name="short_conv",
    )(x, mod, mod, mod, gain, w_in, conv_w, w_out, prev)


def _run_group(x, mod, rows, decode, w, cache=None):
    depth = w["norm_mix"].shape[0]
    n_mix = 2
    new_k, new_v, new_conv, new_ffn = [], [], [], []
    for i in range(depth):
        last = i == depth - 1
        proj = None
        if i % n_mix == 0:
            a = i // n_mix
            if decode:
                q, k, v = _qkv(x, mod, w["norm_mix"], w["attn_w_q"][a], w["attn_w_kv"][a], rows, i,
                               HEAD_DIM ** -0.5, F32, False)
                o = _decode_attention(q, cache["k"], cache["v"], cache["page_table"], w["attn_bias"], a)
            else:
                q, k, v = _qkv(x, mod, w["norm_mix"], w["attn_w_q"][a], w["attn_w_kv_t"][a], rows, i,
                               LOG2E * HEAD_DIM ** -0.5, BF16, True)
                o = _attention(q, k, v, w["attn_bias"], a)
            new_k.append(k)
            new_v.append(v)
            proj = (o, w["attn_w_o"][a])
        else:
            j = i // n_mix
            x, st = _short_conv(x, mod, w["norm_mix"], w["sc_w_in"][j], w["sc_conv_w"][j], w["sc_w_out"][j],
                                w["conv_prev"][j], rows, i)
            new_conv.append(st)
        x, st = _ffn(x, mod, w["norm_ffn"], w["ffn_w_up"][i], w["ffn_conv_w"][i], w["ffn_conv_b"][i],
                     w["ffn_w_down"][i], w["ffn_prev"][i], rows, i, proj=proj,
                     out_gain=w["norm_out"] if last else None)
        new_ffn.append(st)
    return x, new_k, new_v, new_conv, new_ffn


def kernel(x_prompt, x_sample, cache_k, cache_v, state_conv, state_ffn, page_table, c_prompt, c_sample, ada_w, ada_b, norm_mix, norm_ffn, norm_out, attn_w_qkv, attn_w_o, attn_bias, sc_w_in, sc_conv_w, sc_w_out, ffn_w_up, ffn_conv_w, ffn_conv_b, ffn_w_down):
    bp, seq, d = x_prompt.shape
    bs, dec_seq, _ = x_sample.shape
    assert dec_seq == 1, "the sample group decodes one token per sequence"
    depth = ada_w.shape[0]
    f = ffn_w_down.shape[1]
    heads = d // HEAD_DIM
    n_conv = state_conv.shape[0]

    mod_p, mod_s = _ada_mod(c_prompt, c_sample, ada_w, ada_b)
    mod_p = mod_p.reshape(depth, bp, 1, N_MOD * d)

    shared = dict(
        norm_mix=norm_mix.reshape(depth, 1, d), norm_ffn=norm_ffn.reshape(depth, 1, d), norm_out=norm_out,
        attn_w_q=attn_w_qkv[:, :, :d].astype(BF16), attn_w_kv=attn_w_qkv[:, :, d:].astype(BF16),
        attn_w_kv_t=jnp.swapaxes(attn_w_qkv[:, :, d:], 1, 2).astype(BF16),
        attn_w_o=attn_w_o.astype(BF16), attn_bias=attn_bias,
        sc_w_in=sc_w_in.astype(BF16), sc_conv_w=sc_conv_w, sc_w_out=sc_w_out.astype(BF16),
        ffn_w_up=ffn_w_up.astype(BF16), ffn_conv_w=ffn_conv_w, ffn_conv_b=ffn_conv_b,
        ffn_w_down=ffn_w_down.astype(BF16))

    tm = min(ROW_TILE, seq)
    rows_p = Rows(n_groups=bp, tiles=seq // tm, tm=tm, per_row=False)
    w_p = dict(shared, conv_prev=jnp.zeros((n_conv, bp, 2, d), F32), ffn_prev=jnp.zeros((depth, bp, 2, f), F32))
    y_p, k_p, v_p, conv_p, ffn_p = _run_group(x_prompt.reshape(bp * seq, d), mod_p, rows_p, False, w_p)

    rows_s = Rows(n_groups=1, tiles=1, tm=bs, per_row=True)
    w_s = dict(shared, conv_prev=state_conv.reshape(n_conv, bs, 2 * d), ffn_prev=state_ffn.reshape(depth, bs, 2 * f))
    cache = dict(k=cache_k, v=cache_v, page_table=page_table)
    y_s, k_s, v_s, conv_s, ffn_s = _run_group(x_sample.reshape(bs, d), mod_s, rows_s, True, w_s, cache)

    def heads_of(ts, b, t):
        return jnp.stack(ts).reshape(len(ts), b, t, heads, HEAD_DIM)

    def heads_of_t(ts):
        return jnp.transpose(jnp.stack(ts).reshape(len(ts), bp, heads, HEAD_DIM, seq), (0, 1, 4, 2, 3))

    return (y_p.reshape(bp, seq, d), y_s.reshape(bs, 1, d),
            heads_of_t(k_p), heads_of_t(v_p), heads_of(k_s, bs, 1), heads_of(v_s, bs, 1),
            jnp.stack(conv_p), jnp.stack(conv_s).reshape(n_conv, bs, 2, d),
            jnp.stack(ffn_p), jnp.stack(ffn_s).reshape(depth, bs, 2, f))
```

```python
import functools
from typing import NamedTuple

import jax
import jax.numpy as jnp
from jax import lax
from jax.experimental import pallas as pl
from jax.experimental.pallas import tpu as pltpu

F32 = jnp.float32
BF16 = jnp.bfloat16

NORM_EPS = 1e-6
N_MOD = 6
HEAD_DIM = 64
LANES = 128
SUBLANES = 8
VMEM_LIMIT = 56 * 1024 * 1024

ROW_TILE = 512
FF_CHUNK = 256
ATTN_Q = 512
ATTN_S = 256
PAGES_PER_STEP = 8


class Rows(NamedTuple):
    n_groups: int
    tiles: int
    tm: int
    per_row: bool


def _params(*sem):
    return pltpu.CompilerParams(dimension_semantics=sem, vmem_limit_bytes=VMEM_LIMIT)


def _const_spec(shape, index):
    return pl.BlockSpec(shape, index, pipeline_mode=pl.Buffered(1))


def _x_spec(rows, width):
    return pl.BlockSpec((rows.tm, width), lambda g, t: (g * rows.tiles + t, 0))


def _mod_spec(rows, layer, j, d):
    if rows.per_row:
        return pl.BlockSpec((None, rows.tm, d), lambda g, t: (layer, 0, j))
    return pl.BlockSpec((None, None, 1, d), lambda g, t: (layer, g, 0, j))


def _state_in_spec(rows, width):
    if rows.per_row:
        return pl.BlockSpec((rows.tm, 2 * width), lambda g, t: (0, 0))
    return pl.BlockSpec((None, 2, width), lambda g, t: (g, 0, 0))


def _rms_mod(x, gain, scale, shift):
    y = x * lax.rsqrt(jnp.mean(x * x, axis=-1, keepdims=True) + NORM_EPS)
    return (y * gain) * (1.0 + scale) + shift


def _silu(a):
    return a / (1.0 + jnp.exp(-a))


def _ada_kernel(cp_ref, cs_ref, w_ref, b_ref, mp_ref, ms_ref):
    w = w_ref[...].astype(BF16)
    b = b_ref[...]
    for c_ref, o_ref in ((cp_ref, mp_ref), (cs_ref, ms_ref)):
        s = _silu(c_ref[...]).astype(BF16)
        o_ref[...] = jnp.dot(s, w, preferred_element_type=F32) + b


def _ada_mod(c_prompt, c_sample, ada_w, ada_b):
    depth, d, n = ada_w.shape
    tn = n // 4
    bp, bs = c_prompt.shape[0], c_sample.shape[0]
    return pl.pallas_call(
        _ada_kernel,
        out_shape=(jax.ShapeDtypeStruct((depth, bp, n), F32), jax.ShapeDtypeStruct((depth, bs, n), F32)),
        grid=(depth, n // tn),
        in_specs=[
            pl.BlockSpec((bp, d), lambda i, j: (0, 0)),
            pl.BlockSpec((bs, d), lambda i, j: (0, 0)),
            pl.BlockSpec((None, d, tn), lambda i, j: (i, 0, j)),
            pl.BlockSpec((None, 1, tn), lambda i, j: (i, 0, j)),
        ],
        out_specs=(pl.BlockSpec((None, bp, tn), lambda i, j: (i, 0, j)),
                   pl.BlockSpec((None, bs, tn), lambda i, j: (i, 0, j))),
        compiler_params=_params("arbitrary", "arbitrary"),
        name="ada_mod",
    )(c_prompt, c_sample, ada_w, ada_b.reshape(depth, 1, n))


def _qkv_kernel(x_ref, sh_ref, sc_ref, gain_ref, wq_ref, wkv_ref, q_ref, k_ref, v_ref, *, d, q_scale, time_on_lanes):
    h = _rms_mod(x_ref[...], gain_ref[...], sc_ref[...], sh_ref[...]).astype(BF16)
    q = jnp.dot(h, wq_ref[...], preferred_element_type=F32)
    q_ref[...] = (q * q_scale).astype(q_ref.dtype)
    if time_on_lanes:
        nt = (((1,), (1,)), ((), ()))
        k_ref[...] = lax.dot_general(wkv_ref[0:d, :], h, nt, preferred_element_type=F32)
        v_ref[...] = lax.dot_general(wkv_ref[d:2 * d, :], h, nt, preferred_element_type=F32)
    else:
        k_ref[...] = jnp.dot(h, wkv_ref[:, 0:d], preferred_element_type=F32)
        v_ref[...] = jnp.dot(h, wkv_ref[:, d:2 * d], preferred_element_type=F32)


def _qkv(x, mod, gain, w_q, w_kv, rows, layer, q_scale, q_dtype, time_on_lanes):
    n, d = x.shape
    row_out = pl.BlockSpec((rows.tm, d), lambda g, t: (g * rows.tiles + t, 0))
    if time_on_lanes:
        kv_shape = jax.ShapeDtypeStruct((rows.n_groups, d, rows.tiles * rows.tm), F32)
        kv_out = pl.BlockSpec((None, d, rows.tm), lambda g, t: (g, 0, t))
    else:
        kv_shape = jax.ShapeDtypeStruct((n, d), F32)
        kv_out = row_out
    return pl.pallas_call(
        functools.partial(_qkv_kernel, d=d, q_scale=q_scale, time_on_lanes=time_on_lanes),
        out_shape=(jax.ShapeDtypeStruct((n, d), q_dtype), kv_shape, kv_shape),
        grid=(rows.n_groups, rows.tiles),
        in_specs=[
            _x_spec(rows, d),
            _mod_spec(rows, layer, 0, d),
            _mod_spec(rows, layer, 1, d),
            pl.BlockSpec((None, 1, d), lambda g, t: (layer, 0, 0)),
            _const_spec((d, d), lambda g, t: (0, 0)),
            _const_spec(w_kv.shape, lambda g, t: (0, 0)),
        ],
        out_specs=(row_out, kv_out, kv_out),
        compiler_params=_params("arbitrary", "arbitrary"),
        name="qkv_proj",
    )(x, mod, mod, gain, w_q, w_kv)


LOG2E = 1.4426950408889634


def _softplus(z):
    return jnp.maximum(z, 0.0) + jnp.log(1.0 + jnp.exp(-jnp.abs(z)))


EXP2_ARG_CAP = 64.0


def _softplus2(z2):
    return jnp.maximum(jnp.log(1.0 + jnp.exp2(jnp.minimum(z2, EXP2_ARG_CAP))) * LOG2E, z2)


def _split_bf16(a):
    hi = a.astype(BF16)
    return hi, (a - hi.astype(F32)).astype(BF16)


def _strict_tri(n, later_on_rows):
    r = lax.broadcasted_iota(jnp.int32, (n, n), 0)
    c = lax.broadcasted_iota(jnp.int32, (n, n), 1)
    later = (r > c) if later_on_rows else (c > r)
    return jnp.where(later, -1.0, 0.0).astype(BF16)


def _attn_body(hp, qi, bias_ref, q_ref, kt_ref, vt_ref, o_ref, kb_ref, vb_ref, acc_ref, run_ref, *,
               layer_a, tq, ts, seq, alongside):
    nsub = tq // ts
    own_lo = (0, HEAD_DIM)
    one_lo = (HEAD_DIM, 0)

    @pl.when(qi == 0)
    def _():
        row = lax.broadcasted_iota(jnp.int32, (LANES, 1), 0)
        for h in range(2):
            own = (row >= own_lo[h]) & (row < own_lo[h] + HEAD_DIM)
            ones = jnp.where((row >= one_lo[h]) & (row < one_lo[h] + 2), 1.0, 0.0)
            for j in range(seq // ts):
                kb_ref[h, j] = jnp.where(own, kt_ref[:, j * ts:(j + 1) * ts], ones).astype(BF16)
        for j in range(seq // tq):
            vb_ref[j] = vt_ref[:, j * tq:(j + 1) * tq].astype(BF16)

    alongside()
    lane = lax.broadcasted_iota(jnp.int32, (1, LANES), 1)
    first = lane < HEAD_DIM
    q2 = q_ref[...].astype(F32)
    q_head = []
    for h in range(2):
        own = jnp.where((lane >= own_lo[h]) & (lane < own_lo[h] + HEAD_DIM), 1.0, 0.0)
        b = jnp.full((1, LANES), bias_ref[layer_a, 2 * hp + h] * LOG2E, F32)
        b_hi = b.astype(BF16).astype(F32)
        b_lanes = jnp.where(lane == one_lo[h], b_hi, jnp.where(lane == one_lo[h] + 1, b - b_hi, 0.0))
        q_head.append((q2 * own + b_lanes).astype(BF16))
    tri = _strict_tri(ts, later_on_rows=True)
    causal = lax.broadcasted_iota(jnp.int32, (ts, ts), 1) < lax.broadcasted_iota(jnp.int32, (ts, ts), 0)

    acc_ref[...] = jnp.zeros_like(acc_ref)
    run_ref[...] = jnp.zeros_like(run_ref)

    def sweep(h, r0, nrows, tile, subs):
        qh = q_head[h][r0:r0 + nrows]
        swept = None
        probs = []
        for kc, mask in subs:
            z = jnp.dot(qh, kb_ref[h, tile * nsub + kc], preferred_element_type=F32)
            sp = _softplus2(z)
            log_beta = z - sp
            if mask is not None:
                sp = jnp.where(mask, sp, 0.0)
            later = jnp.dot(sp.astype(BF16), tri, preferred_element_type=F32)
            x = log_beta + later
            if swept is not None:
                x = x + swept
            p = jnp.exp2(x)
            if mask is not None:
                p = jnp.where(mask, p, 0.0)
            probs.append(p.astype(BF16))
            total = later[:, 0:1] - sp[:, 0:1]
            swept = total if swept is None else swept + total
        lo, hi = subs[-1][0], subs[0][0] + 1
        p_all = probs[0] if len(probs) == 1 else jnp.concatenate(probs[::-1], axis=1)
        vs = vb_ref[tile, :, lo * ts:hi * ts]
        pv = lax.dot_general(p_all, vs, (((1,), (1,)), ((), ())), preferred_element_type=F32)
        run = run_ref[h, r0:r0 + nrows, :]
        acc_ref[h, r0:r0 + nrows, :] += jnp.exp2(run) * pv
        run_ref[h, r0:r0 + nrows, :] = run + swept

    for rc in range(nsub):
        subs = [(rc, causal)] + [(kc, None) for kc in reversed(range(rc))]
        for h in range(2):
            sweep(h, rc * ts, ts, qi, subs)

    full = [(kc, None) for kc in reversed(range(nsub))]

    def body(it, carry):
        for tile in (qi - 1 - 2 * it, qi - 2 - 2 * it):
            for h in range(2):
                sweep(h, 0, tq, tile, full)
        return carry

    lax.fori_loop(0, lax.shift_right_logical(qi, 1), body, 0)

    @pl.when((qi & 1) == 1)
    def _():
        for h in range(2):
            sweep(h, 0, tq, 0, full)

    o_ref[...] = jnp.where(first, acc_ref[0], acc_ref[1]).astype(o_ref.dtype)


def _dec_start(c, q_ref, qh_ref, acc_ref, run_ref):
    heads, d = qh_ref.shape

    @pl.when(c == 0)
    def _():
        row = lax.broadcasted_iota(jnp.int32, (heads, d), 0)
        col = lax.broadcasted_iota(jnp.int32, (heads, d), 1)
        qh_ref[...] = (jnp.where((col // (d // heads)) == row, 1.0, 0.0) * q_ref[...]).astype(BF16)
        acc_ref[...] = jnp.zeros_like(acc_ref)
        run_ref[...] = jnp.zeros_like(run_ref)


def _dec_finish(c, last_c, o_ref, acc_ref):
    @pl.when(c == last_c)
    def _():
        o_ref[...] = jnp.sum(acc_ref[...].T, axis=0, keepdims=True).astype(o_ref.dtype)


def _dec_step(bias_ref, k_refs, v_refs, qh_ref, p_ref, acc_ref, run_ref):
    pps = len(k_refs)
    d, page = k_refs[0].shape
    heads = qh_ref.shape[0]
    hd = d // heads
    nrow = heads * pps
    q_heads = qh_ref[...]
    z = jnp.concatenate([jnp.dot(q_heads, k_refs[i][...].astype(BF16), preferred_element_type=F32)
                         for i in range(pps)], axis=0) + bias_ref[...]
    sp = _softplus(z)
    r_idx = lax.broadcasted_iota(jnp.int32, (nrow, nrow), 0)
    c_idx = lax.broadcasted_iota(jnp.int32, (nrow, nrow), 1)
    same_head = (r_idx % heads) == (c_idx % heads)
    later_pages = jnp.where(same_head, jnp.where(c_idx > r_idx, -1.0, 0.0), 0.0).astype(BF16)
    all_pages = jnp.where(same_head, -1.0, 0.0).astype(BF16)
    tri = _strict_tri(page, later_on_rows=True)

    sp_hi, sp_lo = _split_bf16(sp)
    later = jnp.dot(sp_hi, tri, preferred_element_type=F32) + jnp.dot(sp_lo, tri, preferred_element_type=F32)
    tot_hi, tot_lo = _split_bf16(jnp.broadcast_to(jnp.sum(sp, axis=-1, keepdims=True), (nrow, page)))
    later = later + (jnp.dot(later_pages, tot_hi, preferred_element_type=F32)
                     + jnp.dot(later_pages, tot_lo, preferred_element_type=F32))
    run = run_ref[...]
    p_ref[...] = jnp.exp((z - sp) + later + run)
    run_ref[...] = run + (jnp.dot(all_pages, tot_hi, preferred_element_type=F32)
                          + jnp.dot(all_pages, tot_lo, preferred_element_type=F32))

    for h in range(heads):
        hs = slice(h * hd, (h + 1) * hd)
        part = None
        for i in range(pps):
            r = i * heads + h
            term = p_ref[r:r + 1, :] * v_refs[i][hs, :]
            part = term if part is None else part + term
        acc_ref[hs, :] += part


def _attention_kernel(pages_ref, seq_ref, c_ref, bias_ref, q_ref, kt_ref, vt_ref, qs_ref, bias_rows_ref, *refs,
                      pps, dec_steps, steps_per_seq, grid, attn):
    k_refs, v_refs = refs[:pps], refs[pps:2 * pps]
    o_ref, os_ref, kb_ref, vb_ref, acc_ref, run_ref, qh_ref, p_ref, dacc_ref, drun_ref = refs[2 * pps:]
    hp, qi = pl.program_id(1), pl.program_id(2)
    step = (pl.program_id(0) * grid[1] + hp) * grid[2] + qi

    c = c_ref[step]
    every_step = dec_steps == grid[0] * grid[1] * grid[2]
    active = None if every_step else step < dec_steps

    def guarded(fn):
        return fn() if every_step else pl.when(active)(fn)

    def decode():
        guarded(lambda: _dec_step(bias_rows_ref, k_refs, v_refs, qh_ref, p_ref, dacc_ref, drun_ref))

    guarded(lambda: _dec_start(c, qs_ref, qh_ref, dacc_ref, drun_ref))
    _attn_body(hp, qi, bias_ref, q_ref, kt_ref, vt_ref, o_ref, kb_ref, vb_ref, acc_ref, run_ref,
               alongside=decode, **attn)
    guarded(lambda: _dec_finish(c, steps_per_seq - 1, os_ref, dacc_ref))


def _attention(q, kt, vt, q_s, cache_k, cache_v, page_table, attn_bias, layer_a):
    n, d = q.shape
    batch, _, seq = kt.shape
    bs = q_s.shape[0]
    n_attn, n_pool, page, heads, hd = cache_k.shape
    assert page == LANES, "one cache page must fill the lane axis"
    tq = min(ATTN_Q, seq)
    ts = min(ATTN_S, tq)
    nq = seq // tq
    n_hp = d // LANES
    grid = (batch, n_hp, nq)
    grid_steps = batch * n_hp * nq

    n_pages = page_table.shape[1]
    fits = [p for p in range(1, n_pages + 1) if n_pages % p == 0 and bs * (n_pages // p) <= grid_steps]
    assert fits, "prompt grid too small to carry the cache sweep"
    pps = min([p for p in fits if p >= PAGES_PER_STEP] or fits[-1:])
    steps_per_seq = n_pages // pps
    dec_steps = bs * steps_per_seq
    nrow = heads * pps

    ck = jnp.transpose(cache_k, (0, 1, 3, 4, 2)).reshape(n_attn * n_pool, d, page)
    cv = jnp.transpose(cache_v, (0, 1, 3, 4, 2)).reshape(n_attn * n_pool, d, page)
    bias_rows = jnp.broadcast_to(attn_bias[layer_a][None, :, None], (pps, heads, page)).reshape(nrow, page)

    step_ids = jnp.minimum(jnp.arange(grid_steps, dtype=jnp.int32), dec_steps - 1)
    seq_of = step_ids // steps_per_seq
    c_of = step_ids % steps_per_seq
    cols = (steps_per_seq - 1 - c_of)[:, None] * pps + jnp.arange(pps, dtype=jnp.int32)[None, :]
    pages = (layer_a * n_pool + page_table[seq_of[:, None], cols]).reshape(grid_steps * pps)

    def step_of(b, hp, qi):
        return (b * n_hp + hp) * nq + qi

    def seq_spec(b, hp, qi, pages_ref, seq_ref, c_ref):
        return seq_ref[step_of(b, hp, qi)], 0, 0

    def page_spec(i):
        return pl.BlockSpec((None, d, page),
                            lambda b, hp, qi, pages_ref, seq_ref, c_ref: (pages_ref[step_of(b, hp, qi) * pps + i], 0, 0))

    qo = pl.BlockSpec((tq, LANES), lambda b, hp, qi, *_: (b * nq + qi, hp))
    kv = pl.BlockSpec((None, LANES, seq), lambda b, hp, qi, *_: (b, hp, 0))
    grid_spec = pltpu.PrefetchScalarGridSpec(
        num_scalar_prefetch=3,
        grid=grid,
        in_specs=[pl.BlockSpec(memory_space=pltpu.SMEM), qo, kv, kv,
                  pl.BlockSpec((None, 1, d), seq_spec),
                  pl.BlockSpec((nrow, page), lambda b, hp, qi, *_: (0, 0))]
                 + [page_spec(i) for i in range(pps)] + [page_spec(i) for i in range(pps)],
        out_specs=(qo, pl.BlockSpec((None, 1, d), seq_spec)),
        scratch_shapes=[pltpu.VMEM((2, seq // ts, LANES, ts), BF16), pltpu.VMEM((seq // tq, LANES, tq), BF16),
                        pltpu.VMEM((2, tq, LANES), F32), pltpu.VMEM((2, tq, LANES), F32),
                        pltpu.VMEM((heads, d), BF16), pltpu.VMEM((nrow, page), F32),
                        pltpu.VMEM((d, page), F32), pltpu.VMEM((nrow, page), F32)],
    )
    o, o_s = pl.pallas_call(
        functools.partial(_attention_kernel, pps=pps, dec_steps=dec_steps,
                          steps_per_seq=steps_per_seq, grid=grid,
                          attn=dict(layer_a=layer_a, tq=tq, ts=ts, seq=seq)),
        out_shape=(jax.ShapeDtypeStruct((n, d), BF16), jax.ShapeDtypeStruct((bs, 1, d), BF16)),
        grid_spec=grid_spec,
        compiler_params=_params("arbitrary", "arbitrary", "arbitrary"),
        name="sb_attention",
    )(pages, seq_of, c_of, attn_bias, q, kt, vt, q_s.reshape(bs, 1, d), bias_rows, *([ck] * pps), *([cv] * pps))
    return o, o_s.reshape(bs, d)


def _conv_carry_in(rows, t, prev_ref, stage_ref):
    if rows.per_row:
        return

    @pl.when(t == 0)
    def _():
        stage_ref[SUBLANES - 2:SUBLANES, :] = prev_ref[...]

    @pl.when(t != 0)
    def _():
        stage_ref[0:SUBLANES, :] = stage_ref[rows.tm:rows.tm + SUBLANES, :]


def _conv3(u, cols, conv_w_ref, rows, prev_ref, stage_ref, state_ref, width):
    tm = rows.tm
    w = conv_w_ref[:, cols]
    if rows.per_row:
        lo = slice(cols.start, cols.stop)
        hi = slice(width + cols.start, width + cols.stop)
        u2, u1 = prev_ref[:, lo], prev_ref[:, hi]
        state_ref[:, lo] = u1
        state_ref[:, hi] = u
    else:
        stage_ref[SUBLANES:SUBLANES + tm, cols] = u
        u1 = stage_ref[SUBLANES - 1:SUBLANES - 1 + tm, cols]
        u2 = stage_ref[SUBLANES - 2:SUBLANES - 2 + tm, cols]
        state_ref[:, cols] = u[tm - 2:tm]
    return w[0:1] * u2 + w[1:2] * u1 + w[2:3] * u


def _ffn_kernel(*refs, rows, d, f, has_proj, final_norm):
    it = iter(refs)
    x_ref = next(it)
    if has_proj:
        a_ref, wp_ref, g1_ref = next(it), next(it), next(it)
    sh_ref, sc_ref, g2_ref, gain_ref = next(it), next(it), next(it), next(it)
    wu_ref, cw_ref, cb_ref, wd_ref, prev_ref = next(it), next(it), next(it), next(it), next(it)
    if final_norm:
        gout_ref = next(it)
    y_ref, state_ref = next(it), next(it)
    act_ref = next(it)
    stage_ref = None if rows.per_row else next(it)
    t = pl.program_id(1)

    x = x_ref[...]
    if has_proj:
        x = x + g1_ref[...] * jnp.dot(a_ref[...], wp_ref[...], preferred_element_type=F32)
    h = _rms_mod(x, gain_ref[...], sc_ref[...], sh_ref[...]).astype(BF16)
    _conv_carry_in(rows, t, prev_ref, stage_ref)
    for c0 in range(0, f, FF_CHUNK):
        cols = slice(c0, c0 + FF_CHUNK)
        g = jnp.dot(h, wu_ref[:, cols], preferred_element_type=F32)
        u = jnp.dot(h, wu_ref[:, f + c0:f + c0 + FF_CHUNK], preferred_element_type=F32)
        a = _conv3(g, cols, cw_ref, rows, prev_ref, stage_ref, state_ref, f) + cb_ref[:, cols]
        act_ref[:, cols] = (_silu(a) * u).astype(BF16)
    x = x + g2_ref[...] * jnp.dot(act_ref[...], wd_ref[...], preferred_element_type=F32)
    if final_norm:
        x = x * lax.rsqrt(jnp.mean(x * x, axis=-1, keepdims=True) + NORM_EPS) * gout_ref[...]
    y_ref[...] = x


def _ffn(x, mod, gain, w_up, conv_w, conv_b, w_down, prev, rows, layer, proj=None, out_gain=None):
    n, d = x.shape
    f = w_down.shape[0]
    const = lambda g, t: (0, 0)
    args, specs = [x], [_x_spec(rows, d)]
    if proj is not None:
        a, w_proj = proj
        args += [a, w_proj, mod]
        specs += [_x_spec(rows, d), _const_spec((d, d), const), _mod_spec(rows, layer, 2, d)]
    args += [mod, mod, mod, gain, w_up, conv_w, conv_b.reshape(1, f), w_down, prev]
    specs += [_mod_spec(rows, layer, 3, d), _mod_spec(rows, layer, 4, d), _mod_spec(rows, layer, 5, d),
              pl.BlockSpec((None, 1, d), lambda g, t: (layer, 0, 0)),
              _const_spec((d, 2 * f), const), _const_spec((3, f), const), _const_spec((1, f), const),
              _const_spec((f, d), const), _state_in_spec(rows, f)]
    if out_gain is not None:
        args.append(out_gain.reshape(1, d))
        specs.append(_const_spec((1, d), const))
    scratch = [pltpu.VMEM((rows.tm, f), BF16)]
    if not rows.per_row:
        scratch.append(pltpu.VMEM((rows.tm + SUBLANES, f), F32))
    state_shape = (n, 2 * f) if rows.per_row else (rows.n_groups, 2, f)
    return pl.pallas_call(
        functools.partial(_ffn_kernel, rows=rows, d=d, f=f, has_proj=proj is not None,
                          final_norm=out_gain is not None),
        out_shape=(jax.ShapeDtypeStruct((n, d), F32), jax.ShapeDtypeStruct(state_shape, F32)),
        grid=(rows.n_groups, rows.tiles),
        in_specs=specs,
        out_specs=(_x_spec(rows, d), _state_in_spec(rows, f)),
        scratch_shapes=scratch,
        compiler_params=_params("arbitrary", "arbitrary"),
        name="conv_ffn",
    )(*args)


def _sconv_kernel(*refs, rows, d):
    x_ref, sh_ref, sc_ref, g1_ref, gain_ref, wi_ref, cw_ref, wo_ref, prev_ref, y_ref, state_ref, gated_ref = refs[:12]
    stage_ref = None if rows.per_row else refs[12]
    t = pl.program_id(1)
    x = x_ref[...]
    h = _rms_mod(x, gain_ref[...], sc_ref[...], sh_ref[...]).astype(BF16)
    _conv_carry_in(rows, t, prev_ref, stage_ref)
    for c0 in range(0, d, FF_CHUNK):
        cols = slice(c0, c0 + FF_CHUNK)
        b_gate = jnp.dot(h, wi_ref[:, cols], preferred_element_type=F32)
        c_gate = jnp.dot(h, wi_ref[:, d + c0:d + c0 + FF_CHUNK], preferred_element_type=F32)
        u = jnp.dot(h, wi_ref[:, 2 * d + c0:2 * d + c0 + FF_CHUNK], preferred_element_type=F32)
        y = _conv3(c_gate * u, cols, cw_ref, rows, prev_ref, stage_ref, state_ref, d)
        gated_ref[:, cols] = (b_gate * y).astype(BF16)
    y_ref[...] = x + g1_ref[...] * jnp.dot(gated_ref[...], wo_ref[...], preferred_element_type=F32)


def _short_conv(x, mod, gain, w_in, conv_w, w_out, prev, rows, layer):
    n, d = x.shape
    const = lambda g, t: (0, 0)
    scratch = [pltpu.VMEM((rows.tm, d), BF16)]
    if not rows.per_row:
        scratch.append(pltpu.VMEM((rows.tm + SUBLANES, d), F32))
    state_shape = (n, 2 * d) if rows.per_row else (rows.n_groups, 2, d)
    return pl.pallas_call(
        functools.partial(_sconv_kernel, rows=rows, d=d),
        out_shape=(jax.ShapeDtypeStruct((n, d), F32), jax.ShapeDtypeStruct(state_shape, F32)),
        grid=(rows.n_groups, rows.tiles),
        in_specs=[_x_spec(rows, d), _mod_spec(rows, layer, 0, d), _mod_spec(rows, layer, 1, d),
                  _mod_spec(rows, layer, 2, d), pl.BlockSpec((None, 1, d), lambda g, t: (layer, 0, 0)),
                  _const_spec((d, 3 * d), const), _const_spec((3, d), const), _const_spec((d, d), const),
                  _state_in_spec(rows, d)],
        out_specs=(_x_spec(rows, d), _state_in_spec(rows, d)),
        scratch_shapes=scratch,
        compiler_params=_params("arbitrary", "arbitrary"),
        name="short_conv",
    )(x, mod, mod, mod, gain, w_in, conv_w, w_out, prev)


class Group(NamedTuple):
    x: jax.Array
    mod: jax.Array
    rows: Rows
    conv_prev: jax.Array
    ffn_prev: jax.Array


def _trunk(prompt, sample, w, cache_k, cache_v, page_table):
    depth = w["norm_mix"].shape[0]
    n_mix = 2
    groups = [prompt, sample]
    new = [dict(k=[], v=[], conv=[], ffn=[]) for _ in groups]
    for i in range(depth):
        last = i == depth - 1
        projs = [None, None]
        if i % n_mix == 0:
            a = i // n_mix
            q_p, k_p, v_p = _qkv(prompt.x, prompt.mod, w["norm_mix"], w["attn_w_q"][a], w["attn_w_kv_t"][a],
                                 prompt.rows, i, LOG2E * HEAD_DIM ** -0.5, BF16, True)
            q_s, k_s, v_s = _qkv(sample.x, sample.mod, w["norm_mix"], w["attn_w_q"][a], w["attn_w_kv"][a],
                                 sample.rows, i, HEAD_DIM ** -0.5, F32, False)
            o_p, o_s = _attention(q_p, k_p, v_p, q_s, cache_k, cache_v, page_table, w["attn_bias"], a)
            for n, k, v, o in ((new[0], k_p, v_p, o_p), (new[1], k_s, v_s, o_s)):
                n["k"].append(k)
                n["v"].append(v)
            projs = [(o_p, w["attn_w_o"][a]), (o_s, w["attn_w_o"][a])]
        else:
            j = i // n_mix
            for gi, g in enumerate(groups):
                x, st = _short_conv(g.x, g.mod, w["norm_mix"], w["sc_w_in"][j], w["sc_conv_w"][j], w["sc_w_out"][j],
                                    g.conv_prev[j], g.rows, i)
                groups[gi] = g._replace(x=x)
                new[gi]["conv"].append(st)
        for gi, g in enumerate(groups):
            x, st = _ffn(g.x, g.mod, w["norm_ffn"], w["ffn_w_up"][i], w["ffn_conv_w"][i], w["ffn_conv_b"][i],
                         w["ffn_w_down"][i], g.ffn_prev[i], g.rows, i, proj=projs[gi],
                         out_gain=w["norm_out"] if last else None)
            groups[gi] = g._replace(x=x)
            new[gi]["ffn"].append(st)
        prompt, sample = groups
    return (prompt.x, new[0]), (sample.x, new[1])


def kernel(x_prompt, x_sample, cache_k, cache_v, state_conv, state_ffn, page_table, c_prompt, c_sample, ada_w, ada_b, norm_mix, norm_ffn, norm_out, attn_w_qkv, attn_w_o, attn_bias, sc_w_in, sc_conv_w, sc_w_out, ffn_w_up, ffn_conv_w, ffn_conv_b, ffn_w_down):
    bp, seq, d = x_prompt.shape
    bs, dec_seq, _ = x_sample.shape
    assert dec_seq == 1, "the sample group decodes one token per sequence"
    depth = ada_w.shape[0]
    f = ffn_w_down.shape[1]
    heads = d // HEAD_DIM
    n_conv = state_conv.shape[0]

    mod_p, mod_s = _ada_mod(c_prompt, c_sample, ada_w, ada_b)
    mod_p = mod_p.reshape(depth, bp, 1, N_MOD * d)

    shared = dict(
        norm_mix=norm_mix.reshape(depth, 1, d), norm_ffn=norm_ffn.reshape(depth, 1, d), norm_out=norm_out,
        attn_w_q=attn_w_qkv[:, :, :d].astype(BF16), attn_w_kv=attn_w_qkv[:, :, d:].astype(BF16),
        attn_w_kv_t=jnp.swapaxes(attn_w_qkv[:, :, d:], 1, 2).astype(BF16),
        attn_w_o=attn_w_o.astype(BF16), attn_bias=attn_bias,
        sc_w_in=sc_w_in.astype(BF16), sc_conv_w=sc_conv_w, sc_w_out=sc_w_out.astype(BF16),
        ffn_w_up=ffn_w_up.astype(BF16), ffn_conv_w=ffn_conv_w, ffn_conv_b=ffn_conv_b,
        ffn_w_down=ffn_w_down.astype(BF16))

    tm = min(ROW_TILE, seq)
    prompt = Group(x_prompt.reshape(bp * seq, d), mod_p, Rows(n_groups=bp, tiles=seq // tm, tm=tm, per_row=False),
                   jnp.zeros((n_conv, bp, 2, d), F32), jnp.zeros((depth, bp, 2, f), F32))
    sample = Group(x_sample.reshape(bs, d), mod_s, Rows(n_groups=1, tiles=1, tm=bs, per_row=True),
                   state_conv.reshape(n_conv, bs, 2 * d), state_ffn.reshape(depth, bs, 2 * f))
    (y_p, new_p), (y_s, new_s) = _trunk(prompt, sample, shared, cache_k, cache_v, page_table)

    def heads_of(ts, b, t):
        return jnp.stack(ts).reshape(len(ts), b, t, heads, HEAD_DIM)

    def heads_of_t(ts):
        return jnp.transpose(jnp.stack(ts).reshape(len(ts), bp, heads, HEAD_DIM, seq), (0, 1, 4, 2, 3))

    return (y_p.reshape(bp, seq, d), y_s.reshape(bs, 1, d),
            heads_of_t(new_p["k"]), heads_of_t(new_p["v"]), heads_of(new_s["k"], bs, 1), heads_of(new_s["v"], bs, 1),
            jnp.stack(new_p["conv"]), jnp.stack(new_s["conv"]).reshape(n_conv, bs, 2, d),
            jnp.stack(new_p["ffn"]), jnp.stack(new_s["ffn"]).reshape(depth, bs, 2, f))
```

```python
import functools
from typing import NamedTuple

import jax
import jax.numpy as jnp
from jax import lax
from jax.experimental import pallas as pl
from jax.experimental.pallas import tpu as pltpu

F32 = jnp.float32
BF16 = jnp.bfloat16

NORM_EPS = 1e-6
N_MOD = 6
HEAD_DIM = 64
LANES = 128
SUBLANES = 8
VMEM_LIMIT = 56 * 1024 * 1024

ROW_TILE = 512
FF_CHUNK = 256
ATTN_Q = 512
ATTN_S = 256
PAGES_PER_STEP = 8


class Rows(NamedTuple):
    n_groups: int
    tiles: int
    tm: int
    per_row: bool


def _params(*sem):
    return pltpu.CompilerParams(dimension_semantics=sem, vmem_limit_bytes=VMEM_LIMIT)


def _const_spec(shape, index):
    return pl.BlockSpec(shape, index, pipeline_mode=pl.Buffered(1))


def _x_spec(rows, width):
    return pl.BlockSpec((rows.tm, width), lambda g, t: (g * rows.tiles + t, 0))


def _mod_spec(rows, layer, j, d):
    if rows.per_row:
        return pl.BlockSpec((None, rows.tm, d), lambda g, t: (layer, 0, j))
    return pl.BlockSpec((None, None, 1, d), lambda g, t: (layer, g, 0, j))


def _state_in_spec(rows, width):
    if rows.per_row:
        return pl.BlockSpec((rows.tm, 2 * width), lambda g, t: (0, 0))
    return pl.BlockSpec((None, 2, width), lambda g, t: (g, 0, 0))


def _rms_mod(x, gain, scale, shift):
    y = x * lax.rsqrt(jnp.mean(x * x, axis=-1, keepdims=True) + NORM_EPS)
    return (y * gain) * (1.0 + scale) + shift


def _silu(a):
    return a / (1.0 + jnp.exp(-a))


def _ada_kernel(cp_ref, cs_ref, w_ref, b_ref, mp_ref, ms_ref):
    w = w_ref[...].astype(BF16)
    b = b_ref[...]
    for c_ref, o_ref in ((cp_ref, mp_ref), (cs_ref, ms_ref)):
        s = _silu(c_ref[...]).astype(BF16)
        o_ref[...] = jnp.dot(s, w, preferred_element_type=F32) + b


def _ada_mod(c_prompt, c_sample, ada_w, ada_b):
    depth, d, n = ada_w.shape
    tn = n // 4
    bp, bs = c_prompt.shape[0], c_sample.shape[0]
    return pl.pallas_call(
        _ada_kernel,
        out_shape=(jax.ShapeDtypeStruct((depth, bp, n), F32), jax.ShapeDtypeStruct((depth, bs, n), F32)),
        grid=(depth, n // tn),
        in_specs=[
            pl.BlockSpec((bp, d), lambda i, j: (0, 0)),
            pl.BlockSpec((bs, d), lambda i, j: (0, 0)),
            pl.BlockSpec((None, d, tn), lambda i, j: (i, 0, j)),
            pl.BlockSpec((None, 1, tn), lambda i, j: (i, 0, j)),
        ],
        out_specs=(pl.BlockSpec((None, bp, tn), lambda i, j: (i, 0, j)),
                   pl.BlockSpec((None, bs, tn), lambda i, j: (i, 0, j))),
        compiler_params=_params("arbitrary", "arbitrary"),
        name="ada_mod",
    )(c_prompt, c_sample, ada_w, ada_b.reshape(depth, 1, n))


def _qkv_kernel(x_ref, sh_ref, sc_ref, gain_ref, wq_ref, wkv_ref, q_ref, k_ref, v_ref, *, d, q_scale, time_on_lanes):
    h = _rms_mod(x_ref[...], gain_ref[...], sc_ref[...], sh_ref[...]).astype(BF16)
    q = jnp.dot(h, wq_ref[...], preferred_element_type=F32)
    q_ref[...] = (q * q_scale).astype(q_ref.dtype)
    if time_on_lanes:
        nt = (((1,), (1,)), ((), ()))
        k_ref[...] = lax.dot_general(wkv_ref[0:d, :], h, nt, preferred_element_type=F32)
        v_ref[...] = lax.dot_general(wkv_ref[d:2 * d, :], h, nt, preferred_element_type=F32)
    else:
        k_ref[...] = jnp.dot(h, wkv_ref[:, 0:d], preferred_element_type=F32)
        v_ref[...] = jnp.dot(h, wkv_ref[:, d:2 * d], preferred_element_type=F32)


def _qkv(x, mod, gain, w_q, w_kv, rows, layer, q_scale, q_dtype, time_on_lanes):
    n, d = x.shape
    row_out = pl.BlockSpec((rows.tm, d), lambda g, t: (g * rows.tiles + t, 0))
    if time_on_lanes:
        kv_shape = jax.ShapeDtypeStruct((rows.n_groups, d, rows.tiles * rows.tm), F32)
        kv_out = pl.BlockSpec((None, d, rows.tm), lambda g, t: (g, 0, t))
    else:
        kv_shape = jax.ShapeDtypeStruct((n, d), F32)
        kv_out = row_out
    return pl.pallas_call(
        functools.partial(_qkv_kernel, d=d, q_scale=q_scale, time_on_lanes=time_on_lanes),
        out_shape=(jax.ShapeDtypeStruct((n, d), q_dtype), kv_shape, kv_shape),
        grid=(rows.n_groups, rows.tiles),
        in_specs=[
            _x_spec(rows, d),
            _mod_spec(rows, layer, 0, d),
            _mod_spec(rows, layer, 1, d),
            pl.BlockSpec((None, 1, d), lambda g, t: (layer, 0, 0)),
            _const_spec((d, d), lambda g, t: (0, 0)),
            _const_spec(w_kv.shape, lambda g, t: (0, 0)),
        ],
        out_specs=(row_out, kv_out, kv_out),
        compiler_params=_params("arbitrary", "arbitrary"),
        name="qkv_proj",
    )(x, mod, mod, gain, w_q, w_kv)


LOG2E = 1.4426950408889634


def _softplus(z):
    return jnp.maximum(z, 0.0) + jnp.log(1.0 + jnp.exp(-jnp.abs(z)))


EXP2_ARG_CAP = 64.0


def _softplus2(z2):
    return jnp.maximum(jnp.log(1.0 + jnp.exp2(jnp.minimum(z2, EXP2_ARG_CAP))) * LOG2E, z2)


def _split_bf16(a):
    hi = a.astype(BF16)
    return hi, (a - hi.astype(F32)).astype(BF16)


def _strict_tri(n, later_on_rows):
    r = lax.broadcasted_iota(jnp.int32, (n, n), 0)
    c = lax.broadcasted_iota(jnp.int32, (n, n), 1)
    later = (r > c) if later_on_rows else (c > r)
    return jnp.where(later, -1.0, 0.0).astype(BF16)


def _attn_body(hp, qi, bias_ref, q_ref, kt_ref, vt_ref, o_ref, kb_ref, vb_ref, acc_ref, run_ref, *,
               layer_a, tq, ts, seq, alongside):
    nsub = tq // ts
    own_lo = (0, HEAD_DIM)
    one_lo = (HEAD_DIM, 0)

    @pl.when(qi == 0)
    def _():
        row = lax.broadcasted_iota(jnp.int32, (LANES, 1), 0)
        for h in range(2):
            own = (row >= own_lo[h]) & (row < own_lo[h] + HEAD_DIM)
            ones = jnp.where((row >= one_lo[h]) & (row < one_lo[h] + 2), 1.0, 0.0)
            for j in range(seq // ts):
                kb_ref[h, j] = jnp.where(own, kt_ref[:, j * ts:(j + 1) * ts], ones).astype(BF16)
        for j in range(seq // tq):
            vb_ref[j] = vt_ref[:, j * tq:(j + 1) * tq].astype(BF16)

    lane = lax.broadcasted_iota(jnp.int32, (1, LANES), 1)
    first = lane < HEAD_DIM
    q2 = q_ref[...].astype(F32)

    def query_with_bias(h, bias):
        own = jnp.where((lane >= own_lo[h]) & (lane < own_lo[h] + HEAD_DIM), 1.0, 0.0)
        b = jnp.full((1, LANES), bias, F32)
        b_hi = b.astype(BF16).astype(F32)
        b_lanes = jnp.where(lane == one_lo[h], b_hi, jnp.where(lane == one_lo[h] + 1, b - b_hi, 0.0))
        return (q2 * own + b_lanes).astype(BF16)

    q_head = [query_with_bias(h, bias_ref[layer_a, 2 * hp + h] * LOG2E) for h in range(2)]
    tri = _strict_tri(ts, later_on_rows=True)
    causal = lax.broadcasted_iota(jnp.int32, (ts, ts), 1) < lax.broadcasted_iota(jnp.int32, (ts, ts), 0)

    acc_ref[...] = jnp.zeros_like(acc_ref)
    run_ref[...] = jnp.zeros_like(run_ref)

    def sweep(h, r0, nrows, tile, subs):
        qh = q_head[h][r0:r0 + nrows]
        swept = None
        probs = []
        for kc, mask in subs:
            z = jnp.dot(qh, kb_ref[h, tile * nsub + kc], preferred_element_type=F32)
            sp = _softplus2(z)
            log_beta = z - sp
            if mask is not None:
                sp = jnp.where(mask, sp, 0.0)
            later = jnp.dot(sp.astype(BF16), tri, preferred_element_type=F32)
            x = log_beta + later
            if swept is not None:
                x = x + swept
            p = jnp.exp2(x)
            if mask is not None:
                p = jnp.where(mask, p, 0.0)
            probs.append(p.astype(BF16))
            total = later[:, 0:1] - sp[:, 0:1]
            swept = total if swept is None else swept + total
        lo, hi = subs[-1][0], subs[0][0] + 1
        p_all = probs[0] if len(probs) == 1 else jnp.concatenate(probs[::-1], axis=1)
        vs = vb_ref[tile, :, lo * ts:hi * ts]
        pv = lax.dot_general(p_all, vs, (((1,), (1,)), ((), ())), preferred_element_type=F32)
        run = run_ref[h, r0:r0 + nrows, :]
        acc_ref[h, r0:r0 + nrows, :] += jnp.exp2(run) * pv
        run_ref[h, r0:r0 + nrows, :] = run + swept

    for rc in range(nsub):
        subs = [(rc, causal)] + [(kc, None) for kc in reversed(range(rc))]
        for h in range(2):
            sweep(h, rc * ts, ts, qi, subs)
    alongside()

    full = [(kc, None) for kc in reversed(range(nsub))]

    def full_tile(tile):
        for h in range(2):
            sweep(h, 0, tq, tile, full)

    def body(it, carry):
        full_tile(qi - 1 - 2 * it)
        full_tile(qi - 2 - 2 * it)
        return carry

    lax.fori_loop(0, lax.shift_right_logical(qi, 1), body, 0)

    @pl.when((qi & 1) == 1)
    def _():
        full_tile(0)

    o_ref[...] = jnp.where(first, acc_ref[0], acc_ref[1]).astype(o_ref.dtype)


def _own_columns(heads, d):
    row = lax.broadcasted_iota(jnp.int32, (heads, d), 0)
    col = lax.broadcasted_iota(jnp.int32, (heads, d), 1)
    return jnp.where((col // (d // heads)) == row, 1.0, 0.0)


def _dec_start(c, q_ref, qh_ref, acc_ref, run_ref):
    @pl.when(c == 0)
    def _():
        qh_ref[...] = (_own_columns(*qh_ref.shape) * q_ref[...]).astype(BF16)
        acc_ref[...] = jnp.zeros_like(acc_ref)
        run_ref[...] = jnp.zeros_like(run_ref)


def _dec_finish(c, last_c, o_ref, acc_ref):
    @pl.when(c == last_c)
    def _():
        o_ref[...] = jnp.sum(acc_ref[...].T, axis=0, keepdims=True).astype(o_ref.dtype)


def _dec_step(bias_ref, k_refs, v_refs, qh_ref, p_ref, acc_ref, run_ref):
    pps = len(k_refs)
    d, page = k_refs[0].shape
    heads = qh_ref.shape[0]
    hd = d // heads
    q_heads = qh_ref[...]
    z = jnp.concatenate([jnp.dot(q_heads, k_refs[i][...].astype(BF16), preferred_element_type=F32)
                         for i in range(pps)], axis=0) + bias_ref[...]
    sp = _softplus(z)
    tri = _strict_tri(page, later_on_rows=True)
    sp_hi, sp_lo = _split_bf16(sp)
    later = jnp.dot(sp_hi, tri, preferred_element_type=F32) + jnp.dot(sp_lo, tri, preferred_element_type=F32)

    total = jnp.sum(sp, axis=-1, keepdims=True)
    run = run_ref[...]
    later_pages = [None] * pps
    for i in reversed(range(pps)):
        later_pages[i] = run
        run = run - total[i * heads:(i + 1) * heads]
    run_ref[...] = run
    p_ref[...] = jnp.exp((z - sp) + later + jnp.concatenate(later_pages, axis=0))

    for h in range(heads):
        hs = slice(h * hd, (h + 1) * hd)
        part = None
        for i in range(pps):
            r = i * heads + h
            term = p_ref[r:r + 1, :] * v_refs[i][hs, :]
            part = term if part is None else part + term
        acc_ref[hs, :] += part


def _attention_kernel(pages_ref, seq_ref, c_ref, bias_ref, q_ref, kt_ref, vt_ref, qs_ref, bias_rows_ref, *refs,
                      pps, dec_steps, steps_per_seq, grid, attn):
    k_refs, v_refs = refs[:pps], refs[pps:2 * pps]
    o_ref, os_ref, kb_ref, vb_ref, acc_ref, run_ref, qh_ref, p_ref, dacc_ref, drun_ref = refs[2 * pps:]
    hp, qi = pl.program_id(1), pl.program_id(2)
    step = (pl.program_id(0) * grid[1] + hp) * grid[2] + qi

    c = c_ref[step]
    every_step = dec_steps == grid[0] * grid[1] * grid[2]
    active = None if every_step else step < dec_steps

    def guarded(fn):
        return fn() if every_step else pl.when(active)(fn)

    def decode():
        guarded(lambda: _dec_step(bias_rows_ref, k_refs, v_refs, qh_ref, p_ref, dacc_ref, drun_ref))

    guarded(lambda: _dec_start(c, qs_ref, qh_ref, dacc_ref, drun_ref))
    _attn_body(hp, qi, bias_ref, q_ref, kt_ref, vt_ref, o_ref, kb_ref, vb_ref, acc_ref, run_ref,
               alongside=decode, **attn)
    guarded(lambda: _dec_finish(c, steps_per_seq - 1, os_ref, dacc_ref))


def _attention(q, kt, vt, q_s, cache_k, cache_v, page_table, attn_bias, layer_a):
    n, d = q.shape
    batch, _, seq = kt.shape
    bs = q_s.shape[0]
    n_attn, n_pool, page, heads, _ = cache_k.shape
    assert page == LANES, "one cache page must fill the lane axis"
    tq = min(ATTN_Q, seq)
    ts = min(ATTN_S, tq)
    nq = seq // tq
    n_hp = d // LANES
    grid = (batch, n_hp, nq)
    grid_steps = batch * n_hp * nq

    n_pages = page_table.shape[1]
    fits = [p for p in range(1, n_pages + 1) if n_pages % p == 0 and bs * (n_pages // p) <= grid_steps]
    assert fits, "prompt grid too small to carry the cache sweep"
    pps = min([p for p in fits if p >= PAGES_PER_STEP] or fits[-1:])
    steps_per_seq = n_pages // pps
    dec_steps = bs * steps_per_seq
    nrow = heads * pps

    ck = jnp.transpose(cache_k, (0, 1, 3, 4, 2)).reshape(n_attn * n_pool, d, page)
    cv = jnp.transpose(cache_v, (0, 1, 3, 4, 2)).reshape(n_attn * n_pool, d, page)
    bias_rows = jnp.broadcast_to(attn_bias[layer_a][None, :, None], (pps, heads, page)).reshape(nrow, page)

    step_ids = jnp.minimum(jnp.arange(grid_steps, dtype=jnp.int32), dec_steps - 1)
    seq_of = step_ids // steps_per_seq
    c_of = step_ids % steps_per_seq
    cols = (steps_per_seq - 1 - c_of)[:, None] * pps + jnp.arange(pps, dtype=jnp.int32)[None, :]
    pages = (layer_a * n_pool + page_table[seq_of[:, None], cols]).reshape(grid_steps * pps)

    def step_of(b, hp, qi):
        return (b * n_hp + hp) * nq + qi

    def seq_spec(b, hp, qi, pages_ref, seq_ref, c_ref):
        return seq_ref[step_of(b, hp, qi)], 0, 0

    def page_spec(i):
        return pl.BlockSpec((None, d, page),
                            lambda b, hp, qi, pages_ref, seq_ref, c_ref: (pages_ref[step_of(b, hp, qi) * pps + i], 0, 0))

    qo = pl.BlockSpec((tq, LANES), lambda b, hp, qi, *_: (b * nq + qi, hp))
    kv = pl.BlockSpec((None, LANES, seq), lambda b, hp, qi, *_: (b, hp, 0))
    grid_spec = pltpu.PrefetchScalarGridSpec(
        num_scalar_prefetch=3,
        grid=grid,
        in_specs=[pl.BlockSpec(memory_space=pltpu.SMEM), qo, kv, kv,
                  pl.BlockSpec((None, 1, d), seq_spec),
                  pl.BlockSpec((nrow, page), lambda b, hp, qi, *_: (0, 0))]
                 + [page_spec(i) for i in range(pps)] + [page_spec(i) for i in range(pps)],
        out_specs=(qo, pl.BlockSpec((None, 1, d), seq_spec)),
        scratch_shapes=[pltpu.VMEM((2, seq // ts, LANES, ts), BF16), pltpu.VMEM((seq // tq, LANES, tq), BF16),
                        pltpu.VMEM((2, tq, LANES), F32), pltpu.VMEM((2, tq, LANES), F32),
                        pltpu.VMEM((heads, d), BF16), pltpu.VMEM((nrow, page), F32),
                        pltpu.VMEM((d, page), F32), pltpu.VMEM((heads, page), F32)],
    )
    o, o_s = pl.pallas_call(
        functools.partial(_attention_kernel, pps=pps, dec_steps=dec_steps,
                          steps_per_seq=steps_per_seq, grid=grid,
                          attn=dict(layer_a=layer_a, tq=tq, ts=ts, seq=seq)),
        out_shape=(jax.ShapeDtypeStruct((n, d), BF16), jax.ShapeDtypeStruct((bs, 1, d), BF16)),
        grid_spec=grid_spec,
        compiler_params=_params("arbitrary", "arbitrary", "arbitrary"),
        name="sb_attention",
    )(pages, seq_of, c_of, attn_bias, q, kt, vt, q_s.reshape(bs, 1, d), bias_rows, *([ck] * pps), *([cv] * pps))
    return o, o_s.reshape(bs, d)


def _conv_carry_in(rows, t, prev_ref, stage_ref):
    if rows.per_row:
        return

    @pl.when(t == 0)
    def _():
        stage_ref[SUBLANES - 2:SUBLANES, :] = prev_ref[...]

    @pl.when(t != 0)
    def _():
        stage_ref[0:SUBLANES, :] = stage_ref[rows.tm:rows.tm + SUBLANES, :]


def _conv3(u, cols, conv_w_ref, rows, prev_ref, stage_ref, state_ref, width):
    tm = rows.tm
    w = conv_w_ref[:, cols]
    if rows.per_row:
        lo = slice(cols.start, cols.stop)
        hi = slice(width + cols.start, width + cols.stop)
        u2, u1 = prev_ref[:, lo], prev_ref[:, hi]
        state_ref[:, lo] = u1
        state_ref[:, hi] = u
    else:
        stage_ref[SUBLANES:SUBLANES + tm, cols] = u
        u1 = stage_ref[SUBLANES - 1:SUBLANES - 1 + tm, cols]
        u2 = stage_ref[SUBLANES - 2:SUBLANES - 2 + tm, cols]
        state_ref[:, cols] = u[tm - 2:tm]
    return w[0:1] * u2 + w[1:2] * u1 + w[2:3] * u


def _ffn_kernel(*refs, rows, d, f, has_proj, final_norm):
    it = iter(refs)
    x_ref = next(it)
    if has_proj:
        a_ref, wp_ref, g1_ref = next(it), next(it), next(it)
    sh_ref, sc_ref, g2_ref, gain_ref = next(it), next(it), next(it), next(it)
    wu_ref, cw_ref, cb_ref, wd_ref, prev_ref = next(it), next(it), next(it), next(it), next(it)
    if final_norm:
        gout_ref = next(it)
    y_ref, state_ref = next(it), next(it)
    act_ref = next(it)
    stage_ref = None if rows.per_row else next(it)
    t = pl.program_id(1)

    x = x_ref[...]
    if has_proj:
        x = x + g1_ref[...] * jnp.dot(a_ref[...], wp_ref[...], preferred_element_type=F32)
    h = _rms_mod(x, gain_ref[...], sc_ref[...], sh_ref[...]).astype(BF16)
    _conv_carry_in(rows, t, prev_ref, stage_ref)
    for c0 in range(0, f, FF_CHUNK):
        cols = slice(c0, c0 + FF_CHUNK)
        g = jnp.dot(h, wu_ref[:, cols], preferred_element_type=F32)
        u = jnp.dot(h, wu_ref[:, f + c0:f + c0 + FF_CHUNK], preferred_element_type=F32)
        a = _conv3(g, cols, cw_ref, rows, prev_ref, stage_ref, state_ref, f) + cb_ref[:, cols]
        act_ref[:, cols] = (_silu(a) * u).astype(BF16)
    x = x + g2_ref[...] * jnp.dot(act_ref[...], wd_ref[...], preferred_element_type=F32)
    if final_norm:
        x = x * lax.rsqrt(jnp.mean(x * x, axis=-1, keepdims=True) + NORM_EPS) * gout_ref[...]
    y_ref[...] = x


def _ffn(x, mod, gain, w_up, conv_w, conv_b, w_down, prev, rows, layer, proj=None, out_gain=None):
    n, d = x.shape
    f = w_down.shape[0]
    const = lambda g, t: (0, 0)
    args, specs = [x], [_x_spec(rows, d)]
    if proj is not None:
        a, w_proj = proj
        args += [a, w_proj, mod]
        specs += [_x_spec(rows, d), _const_spec((d, d), const), _mod_spec(rows, layer, 2, d)]
    args += [mod, mod, mod, gain, w_up, conv_w, conv_b.reshape(1, f), w_down, prev]
    specs += [_mod_spec(rows, layer, 3, d), _mod_spec(rows, layer, 4, d), _mod_spec(rows, layer, 5, d),
              pl.BlockSpec((None, 1, d), lambda g, t: (layer, 0, 0)),
              _const_spec((d, 2 * f), const), _const_spec((3, f), const), _const_spec((1, f), const),
              _const_spec((f, d), const), _state_in_spec(rows, f)]
    if out_gain is not None:
        args.append(out_gain.reshape(1, d))
        specs.append(_const_spec((1, d), const))
    scratch = [pltpu.VMEM((rows.tm, f), BF16)]
    if not rows.per_row:
        scratch.append(pltpu.VMEM((rows.tm + SUBLANES, f), F32))
    state_shape = (n, 2 * f) if rows.per_row else (rows.n_groups, 2, f)
    return pl.pallas_call(
        functools.partial(_ffn_kernel, rows=rows, d=d, f=f, has_proj=proj is not None,
                          final_norm=out_gain is not None),
        out_shape=(jax.ShapeDtypeStruct((n, d), F32), jax.ShapeDtypeStruct(state_shape, F32)),
        grid=(rows.n_groups, rows.tiles),
        in_specs=specs,
        out_specs=(_x_spec(rows, d), _state_in_spec(rows, f)),
        scratch_shapes=scratch,
        compiler_params=_params("arbitrary", "arbitrary"),
        name="conv_ffn",
    )(*args)


def _sconv_kernel(*refs, rows, d):
    x_ref, sh_ref, sc_ref, g1_ref, gain_ref, wi_ref, cw_ref, wo_ref, prev_ref, y_ref, state_ref, gated_ref = refs[:12]
    stage_ref = None if rows.per_row else refs[12]
    t = pl.program_id(1)
    x = x_ref[...]
    h = _rms_mod(x, gain_ref[...], sc_ref[...], sh_ref[...]).astype(BF16)
    _conv_carry_in(rows, t, prev_ref, stage_ref)
    for c0 in range(0, d, FF_CHUNK):
        cols = slice(c0, c0 + FF_CHUNK)
        b_gate = jnp.dot(h, wi_ref[:, cols], preferred_element_type=F32)
        c_gate = jnp.dot(h, wi_ref[:, d + c0:d + c0 + FF_CHUNK], preferred_element_type=F32)
        u = jnp.dot(h, wi_ref[:, 2 * d + c0:2 * d + c0 + FF_CHUNK], preferred_element_type=F32)
        y = _conv3(c_gate * u, cols, cw_ref, rows, prev_ref, stage_ref, state_ref, d)
        gated_ref[:, cols] = (b_gate * y).astype(BF16)
    y_ref[...] = x + g1_ref[...] * jnp.dot(gated_ref[...], wo_ref[...], preferred_element_type=F32)


def _short_conv(x, mod, gain, w_in, conv_w, w_out, prev, rows, layer):
    n, d = x.shape
    const = lambda g, t: (0, 0)
    scratch = [pltpu.VMEM((rows.tm, d), BF16)]
    if not rows.per_row:
        scratch.append(pltpu.VMEM((rows.tm + SUBLANES, d), F32))
    state_shape = (n, 2 * d) if rows.per_row else (rows.n_groups, 2, d)
    return pl.pallas_call(
        functools.partial(_sconv_kernel, rows=rows, d=d),
        out_shape=(jax.ShapeDtypeStruct((n, d), F32), jax.ShapeDtypeStruct(state_shape, F32)),
        grid=(rows.n_groups, rows.tiles),
        in_specs=[_x_spec(rows, d), _mod_spec(rows, layer, 0, d), _mod_spec(rows, layer, 1, d),
                  _mod_spec(rows, layer, 2, d), pl.BlockSpec((None, 1, d), lambda g, t: (layer, 0, 0)),
                  _const_spec((d, 3 * d), const), _const_spec((3, d), const), _const_spec((d, d), const),
                  _state_in_spec(rows, d)],
        out_specs=(_x_spec(rows, d), _state_in_spec(rows, d)),
        scratch_shapes=scratch,
        compiler_params=_params("arbitrary", "arbitrary"),
        name="short_conv",
    )(x, mod, mod, mod, gain, w_in, conv_w, w_out, prev)


class Group(NamedTuple):
    x: jax.Array
    mod: jax.Array
    rows: Rows
    conv_prev: jax.Array
    ffn_prev: jax.Array


def _trunk(prompt, sample, w, cache_k, cache_v, page_table):
    depth = w["norm_mix"].shape[0]
    n_mix = 2
    groups = [prompt, sample]
    new = [dict(k=[], v=[], conv=[], ffn=[]) for _ in groups]
    for i in range(depth):
        last = i == depth - 1
        projs = [None, None]
        if i % n_mix == 0:
            a = i // n_mix
            q_p, k_p, v_p = _qkv(prompt.x, prompt.mod, w["norm_mix"], w["attn_w_q"][a], w["attn_w_kv_t"][a],
                                 prompt.rows, i, LOG2E * HEAD_DIM ** -0.5, BF16, True)
            q_s, k_s, v_s = _qkv(sample.x, sample.mod, w["norm_mix"], w["attn_w_q"][a], w["attn_w_kv"][a],
                                 sample.rows, i, HEAD_DIM ** -0.5, F32, False)
            o_p, o_s = _attention(q_p, k_p, v_p, q_s, cache_k, cache_v, page_table, w["attn_bias"], a)
            for n, k, v in ((new[0], k_p, v_p), (new[1], k_s, v_s)):
                n["k"].append(k)
                n["v"].append(v)
            projs = [(o_p, w["attn_w_o"][a]), (o_s, w["attn_w_o"][a])]
        else:
            j = i // n_mix
            for gi, g in enumerate(groups):
                x, st = _short_conv(g.x, g.mod, w["norm_mix"], w["sc_w_in"][j], w["sc_conv_w"][j], w["sc_w_out"][j],
                                    g.conv_prev[j], g.rows, i)
                groups[gi] = g._replace(x=x)
                new[gi]["conv"].append(st)
        for gi, g in enumerate(groups):
            x, st = _ffn(g.x, g.mod, w["norm_ffn"], w["ffn_w_up"][i], w["ffn_conv_w"][i], w["ffn_conv_b"][i],
                         w["ffn_w_down"][i], g.ffn_prev[i], g.rows, i, proj=projs[gi],
                         out_gain=w["norm_out"] if last else None)
            groups[gi] = g._replace(x=x)
            new[gi]["ffn"].append(st)
        prompt, sample = groups
    return (prompt.x, new[0]), (sample.x, new[1])


def kernel(x_prompt, x_sample, cache_k, cache_v, state_conv, state_ffn, page_table, c_prompt, c_sample, ada_w, ada_b, norm_mix, norm_ffn, norm_out, attn_w_qkv, attn_w_o, attn_bias, sc_w_in, sc_conv_w, sc_w_out, ffn_w_up, ffn_conv_w, ffn_conv_b, ffn_w_down):
    bp, seq, d = x_prompt.shape
    bs, dec_seq, _ = x_sample.shape
    assert dec_seq == 1, "the sample group decodes one token per sequence"
    depth = ada_w.shape[0]
    f = ffn_w_down.shape[1]
    heads = d // HEAD_DIM
    n_conv = state_conv.shape[0]

    mod_p, mod_s = _ada_mod(c_prompt, c_sample, ada_w, ada_b)
    mod_p = mod_p.reshape(depth, bp, 1, N_MOD * d)

    shared = dict(
        norm_mix=norm_mix.reshape(depth, 1, d), norm_ffn=norm_ffn.reshape(depth, 1, d), norm_out=norm_out,
        attn_w_q=attn_w_qkv[:, :, :d].astype(BF16), attn_w_kv=attn_w_qkv[:, :, d:].astype(BF16),
        attn_w_kv_t=jnp.swapaxes(attn_w_qkv[:, :, d:], 1, 2).astype(BF16),
        attn_w_o=attn_w_o.astype(BF16), attn_bias=attn_bias,
        sc_w_in=sc_w_in.astype(BF16), sc_conv_w=sc_conv_w, sc_w_out=sc_w_out.astype(BF16),
        ffn_w_up=ffn_w_up.astype(BF16), ffn_conv_w=ffn_conv_w, ffn_conv_b=ffn_conv_b,
        ffn_w_down=ffn_w_down.astype(BF16))

    tm = min(ROW_TILE, seq)
    prompt = Group(x_prompt.reshape(bp * seq, d), mod_p, Rows(n_groups=bp, tiles=seq // tm, tm=tm, per_row=False),
                   jnp.zeros((n_conv, bp, 2, d), F32), jnp.zeros((depth, bp, 2, f), F32))
    sample = Group(x_sample.reshape(bs, d), mod_s, Rows(n_groups=1, tiles=1, tm=bs, per_row=True),
                   state_conv.reshape(n_conv, bs, 2 * d), state_ffn.reshape(depth, bs, 2 * f))
    (y_p, new_p), (y_s, new_s) = _trunk(prompt, sample, shared, cache_k, cache_v, page_table)

    def heads_of(ts, b, t):
        return jnp.stack(ts).reshape(len(ts), b, t, heads, HEAD_DIM)

    def heads_of_t(ts):
        return jnp.transpose(jnp.stack(ts).reshape(len(ts), bp, heads, HEAD_DIM, seq), (0, 1, 4, 2, 3))

    return (y_p.reshape(bp, seq, d), y_s.reshape(bs, 1, d),
            heads_of_t(new_p["k"]), heads_of_t(new_p["v"]), heads_of(new_s["k"], bs, 1), heads_of(new_s["v"], bs, 1),
            jnp.stack(new_p["conv"]), jnp.stack(new_s["conv"]).reshape(n_conv, bs, 2, d),
            jnp.stack(new_p["ffn"]), jnp.stack(new_s["ffn"]).reshape(depth, bs, 2, f))
```

```python
import functools
from typing import NamedTuple

import jax
import jax.numpy as jnp
from jax import lax
from jax.experimental import pallas as pl
from jax.experimental.pallas import tpu as pltpu

F32 = jnp.float32
BF16 = jnp.bfloat16

NORM_EPS = 1e-6
N_MOD = 6
HEAD_DIM = 64
LANES = 128
SUBLANES = 8
VMEM_LIMIT = 56 * 1024 * 1024

ROW_TILE = 512
FF_CHUNK = 256
ATTN_Q = 512
ATTN_S = 256
PAGES_PER_STEP = 8


class Rows(NamedTuple):
    n_groups: int
    tiles: int
    tm: int
    per_row: bool


def _params(*sem):
    return pltpu.CompilerParams(dimension_semantics=sem, vmem_limit_bytes=VMEM_LIMIT)


def _const_spec(shape, index):
    return pl.BlockSpec(shape, index, pipeline_mode=pl.Buffered(1))


def _x_spec(rows, width):
    return pl.BlockSpec((rows.tm, width), lambda g, t: (g * rows.tiles + t, 0))


def _mod_spec(rows, layer, j, d):
    if rows.per_row:
        return pl.BlockSpec((None, rows.tm, d), lambda g, t: (layer, 0, j))
    return pl.BlockSpec((None, None, 1, d), lambda g, t: (layer, g, 0, j))


def _state_in_spec(rows, width):
    if rows.per_row:
        return pl.BlockSpec((rows.tm, 2 * width), lambda g, t: (0, 0))
    return pl.BlockSpec((None, 2, width), lambda g, t: (g, 0, 0))


def _rms_mod(x, gain, scale, shift):
    y = x * lax.rsqrt(jnp.mean(x * x, axis=-1, keepdims=True) + NORM_EPS)
    return (y * gain) * (1.0 + scale) + shift


def _silu(a):
    return a / (1.0 + jnp.exp(-a))


def _ada_kernel(cp_ref, cs_ref, w_ref, b_ref, mp_ref, ms_ref):
    w = w_ref[...].astype(BF16)
    b = b_ref[...]
    for c_ref, o_ref in ((cp_ref, mp_ref), (cs_ref, ms_ref)):
        s = _silu(c_ref[...]).astype(BF16)
        o_ref[...] = jnp.dot(s, w, preferred_element_type=F32) + b


def _ada_mod(c_prompt, c_sample, ada_w, ada_b):
    depth, d, n = ada_w.shape
    tn = n // 4
    bp, bs = c_prompt.shape[0], c_sample.shape[0]
    return pl.pallas_call(
        _ada_kernel,
        out_shape=(jax.ShapeDtypeStruct((depth, bp, n), F32), jax.ShapeDtypeStruct((depth, bs, n), F32)),
        grid=(depth, n // tn),
        in_specs=[
            pl.BlockSpec((bp, d), lambda i, j: (0, 0)),
            pl.BlockSpec((bs, d), lambda i, j: (0, 0)),
            pl.BlockSpec((None, d, tn), lambda i, j: (i, 0, j)),
            pl.BlockSpec((None, 1, tn), lambda i, j: (i, 0, j)),
        ],
        out_specs=(pl.BlockSpec((None, bp, tn), lambda i, j: (i, 0, j)),
                   pl.BlockSpec((None, bs, tn), lambda i, j: (i, 0, j))),
        compiler_params=_params("arbitrary", "arbitrary"),
        name="ada_mod",
    )(c_prompt, c_sample, ada_w, ada_b.reshape(depth, 1, n))


def _qkv_kernel(x_ref, sh_ref, sc_ref, gain_ref, wq_ref, wkv_ref, q_ref, k_ref, v_ref, *, d, q_scale, time_on_lanes):
    h = _rms_mod(x_ref[...], gain_ref[...], sc_ref[...], sh_ref[...]).astype(BF16)
    q = jnp.dot(h, wq_ref[...], preferred_element_type=F32)
    q_ref[...] = (q * q_scale).astype(q_ref.dtype)
    if time_on_lanes:
        nt = (((1,), (1,)), ((), ()))
        k_ref[...] = lax.dot_general(wkv_ref[0:d, :], h, nt, preferred_element_type=F32)
        v_ref[...] = lax.dot_general(wkv_ref[d:2 * d, :], h, nt, preferred_element_type=F32)
    else:
        k_ref[...] = jnp.dot(h, wkv_ref[:, 0:d], preferred_element_type=F32)
        v_ref[...] = jnp.dot(h, wkv_ref[:, d:2 * d], preferred_element_type=F32)


def _qkv(x, mod, gain, w_q, w_kv, rows, layer, q_scale, q_dtype, time_on_lanes):
    n, d = x.shape
    row_out = pl.BlockSpec((rows.tm, d), lambda g, t: (g * rows.tiles + t, 0))
    if time_on_lanes:
        kv_shape = jax.ShapeDtypeStruct((rows.n_groups, d, rows.tiles * rows.tm), F32)
        kv_out = pl.BlockSpec((None, d, rows.tm), lambda g, t: (g, 0, t))
    else:
        kv_shape = jax.ShapeDtypeStruct((n, d), F32)
        kv_out = row_out
    return pl.pallas_call(
        functools.partial(_qkv_kernel, d=d, q_scale=q_scale, time_on_lanes=time_on_lanes),
        out_shape=(jax.ShapeDtypeStruct((n, d), q_dtype), kv_shape, kv_shape),
        grid=(rows.n_groups, rows.tiles),
        in_specs=[
            _x_spec(rows, d),
            _mod_spec(rows, layer, 0, d),
            _mod_spec(rows, layer, 1, d),
            pl.BlockSpec((None, 1, d), lambda g, t: (layer, 0, 0)),
            _const_spec((d, d), lambda g, t: (0, 0)),
            _const_spec(w_kv.shape, lambda g, t: (0, 0)),
        ],
        out_specs=(row_out, kv_out, kv_out),
        compiler_params=_params("arbitrary", "arbitrary"),
        name="qkv_proj",
    )(x, mod, mod, gain, w_q, w_kv)


LOG2E = 1.4426950408889634


def _softplus(z):
    return jnp.maximum(z, 0.0) + jnp.log(1.0 + jnp.exp(-jnp.abs(z)))


EXP2_ARG_CAP = 64.0


def _softplus2(z2):
    return jnp.maximum(jnp.log(1.0 + jnp.exp2(jnp.minimum(z2, EXP2_ARG_CAP))) * LOG2E, z2)


def _split_bf16(a):
    hi = a.astype(BF16)
    return hi, (a - hi.astype(F32)).astype(BF16)


def _strict_tri(n, later_on_rows):
    r = lax.broadcasted_iota(jnp.int32, (n, n), 0)
    c = lax.broadcasted_iota(jnp.int32, (n, n), 1)
    later = (r > c) if later_on_rows else (c > r)
    return jnp.where(later, -1.0, 0.0).astype(BF16)


def _attn_body(hp, qi, bias_ref, q_ref, kt_ref, vt_ref, o_ref, kb_ref, vb_ref, acc_ref, run_ref, *,
               layer_a, tq, ts, seq, alongside):
    nsub = tq // ts
    own_lo = (0, HEAD_DIM)
    one_lo = (HEAD_DIM, 0)

    @pl.when(qi == 0)
    def _():
        row = lax.broadcasted_iota(jnp.int32, (LANES, 1), 0)
        for h in range(2):
            own = (row >= own_lo[h]) & (row < own_lo[h] + HEAD_DIM)
            ones = jnp.where((row >= one_lo[h]) & (row < one_lo[h] + 2), 1.0, 0.0)
            for j in range(seq // ts):
                kb_ref[h, j] = jnp.where(own, kt_ref[:, j * ts:(j + 1) * ts], ones).astype(BF16)
        for j in range(seq // tq):
            vb_ref[j] = vt_ref[:, j * tq:(j + 1) * tq].astype(BF16)

    lane = lax.broadcasted_iota(jnp.int32, (1, LANES), 1)
    first = lane < HEAD_DIM
    q2 = q_ref[...].astype(F32)

    def query_with_bias(h, bias):
        own = jnp.where((lane >= own_lo[h]) & (lane < own_lo[h] + HEAD_DIM), 1.0, 0.0)
        b = jnp.full((1, LANES), bias, F32)
        b_hi = b.astype(BF16).astype(F32)
        b_lanes = jnp.where(lane == one_lo[h], b_hi, jnp.where(lane == one_lo[h] + 1, b - b_hi, 0.0))
        return (q2 * own + b_lanes).astype(BF16)

    q_head = [query_with_bias(h, bias_ref[layer_a, 2 * hp + h] * LOG2E) for h in range(2)]
    tri = _strict_tri(ts, later_on_rows=True)
    causal = lax.broadcasted_iota(jnp.int32, (ts, ts), 1) < lax.broadcasted_iota(jnp.int32, (ts, ts), 0)

    acc_ref[...] = jnp.zeros_like(acc_ref)
    run_ref[...] = jnp.zeros_like(run_ref)

    def sweep(h, r0, nrows, tile, subs):
        qh = q_head[h][r0:r0 + nrows]
        run = run_ref[h, r0:r0 + nrows, :]
        probs = []
        for kc, mask in subs:
            z = jnp.dot(qh, kb_ref[h, tile * nsub + kc], preferred_element_type=F32)
            sp = _softplus2(z)
            log_beta = z - sp
            if mask is not None:
                sp = jnp.where(mask, sp, 0.0)
            later = jnp.dot(sp.astype(BF16), tri, preferred_element_type=F32)
            p = jnp.exp2(log_beta + later + jnp.concatenate([run] * (ts // LANES), axis=1))
            if mask is not None:
                p = jnp.where(mask, p, 0.0)
            probs.append(p.astype(BF16))
            run = run + (later[:, 0:1] - sp[:, 0:1])
        lo, hi = subs[-1][0], subs[0][0] + 1
        p_all = probs[0] if len(probs) == 1 else jnp.concatenate(probs[::-1], axis=1)
        vs = vb_ref[tile, :, lo * ts:hi * ts]
        acc_ref[h, r0:r0 + nrows, :] += lax.dot_general(p_all, vs, (((1,), (1,)), ((), ())),
                                                        preferred_element_type=F32)
        run_ref[h, r0:r0 + nrows, :] = run

    for rc in range(nsub):
        subs = [(rc, causal)] + [(kc, None) for kc in reversed(range(rc))]
        for h in range(2):
            sweep(h, rc * ts, ts, qi, subs)
    alongside()

    full = [(kc, None) for kc in reversed(range(nsub))]

    def full_tile(tile):
        for h in range(2):
            sweep(h, 0, tq, tile, full)

    def body(it, carry):
        full_tile(qi - 1 - 2 * it)
        full_tile(qi - 2 - 2 * it)
        return carry

    lax.fori_loop(0, lax.shift_right_logical(qi, 1), body, 0)

    @pl.when((qi & 1) == 1)
    def _():
        full_tile(0)

    o_ref[...] = jnp.where(first, acc_ref[0], acc_ref[1]).astype(o_ref.dtype)


def _own_columns(heads, d):
    row = lax.broadcasted_iota(jnp.int32, (heads, d), 0)
    col = lax.broadcasted_iota(jnp.int32, (heads, d), 1)
    return jnp.where((col // (d // heads)) == row, 1.0, 0.0)


def _dec_start(c, q_ref, qh_ref, acc_ref, run_ref):
    @pl.when(c == 0)
    def _():
        qh_ref[...] = (_own_columns(*qh_ref.shape) * q_ref[...]).astype(BF16)
        acc_ref[...] = jnp.zeros_like(acc_ref)
        run_ref[...] = jnp.zeros_like(run_ref)


def _dec_finish(c, last_c, o_ref, acc_ref):
    @pl.when(c == last_c)
    def _():
        o_ref[...] = jnp.sum(acc_ref[...].T, axis=0, keepdims=True).astype(o_ref.dtype)


def _dec_step(bias_ref, k_refs, v_refs, qh_ref, p_ref, acc_ref, run_ref):
    pps = len(k_refs)
    d, page = k_refs[0].shape
    heads = qh_ref.shape[0]
    hd = d // heads
    q_heads = qh_ref[...]
    keys = jnp.concatenate([k_refs[i][...].astype(BF16) for i in range(pps)], axis=1)
    z_wide = jnp.dot(q_heads, keys, preferred_element_type=F32)
    z = jnp.concatenate([z_wide[:, i * page:(i + 1) * page] for i in range(pps)], axis=0) + bias_ref[...]
    sp = _softplus(z)
    tri = _strict_tri(page, later_on_rows=True)
    sp_hi, sp_lo = _split_bf16(sp)
    later = jnp.dot(sp_hi, tri, preferred_element_type=F32) + jnp.dot(sp_lo, tri, preferred_element_type=F32)

    total = jnp.sum(sp, axis=-1, keepdims=True)
    run = run_ref[...]
    later_pages = [None] * pps
    for i in reversed(range(pps)):
        later_pages[i] = run
        run = run - total[i * heads:(i + 1) * heads]
    run_ref[...] = run
    p_ref[...] = jnp.exp((z - sp) + later + jnp.concatenate(later_pages, axis=0))

    for h in range(heads):
        hs = slice(h * hd, (h + 1) * hd)
        part = None
        for i in range(pps):
            r = i * heads + h
            term = p_ref[r:r + 1, :] * v_refs[i][hs, :]
            part = term if part is None else part + term
        acc_ref[hs, :] += part


def _attention_kernel(pages_ref, seq_ref, c_ref, bias_ref, q_ref, kt_ref, vt_ref, qs_ref, bias_rows_ref, *refs,
                      pps, dec_steps, steps_per_seq, grid, attn):
    k_refs, v_refs = refs[:pps], refs[pps:2 * pps]
    o_ref, os_ref, kb_ref, vb_ref, acc_ref, run_ref, qh_ref, p_ref, dacc_ref, drun_ref = refs[2 * pps:]
    hp, qi = pl.program_id(1), pl.program_id(2)
    step = (pl.program_id(0) * grid[1] + hp) * grid[2] + qi

    c = c_ref[step]
    every_step = dec_steps == grid[0] * grid[1] * grid[2]
    active = None if every_step else step < dec_steps

    def guarded(fn):
        return fn() if every_step else pl.when(active)(fn)

    def decode():
        guarded(lambda: _dec_step(bias_rows_ref, k_refs, v_refs, qh_ref, p_ref, dacc_ref, drun_ref))

    guarded(lambda: _dec_start(c, qs_ref, qh_ref, dacc_ref, drun_ref))
    _attn_body(hp, qi, bias_ref, q_ref, kt_ref, vt_ref, o_ref, kb_ref, vb_ref, acc_ref, run_ref,
               alongside=decode, **attn)
    guarded(lambda: _dec_finish(c, steps_per_seq - 1, os_ref, dacc_ref))


def _attention(q, kt, vt, q_s, cache_k, cache_v, page_table, attn_bias, layer_a):
    n, d = q.shape
    batch, _, seq = kt.shape
    bs = q_s.shape[0]
    n_attn, n_pool, page, heads, _ = cache_k.shape
    assert page == LANES, "one cache page must fill the lane axis"
    tq = min(ATTN_Q, seq)
    ts = min(ATTN_S, tq)
    nq = seq // tq
    n_hp = d // LANES
    grid = (batch, n_hp, nq)
    grid_steps = batch * n_hp * nq

    n_pages = page_table.shape[1]
    fits = [p for p in range(1, n_pages + 1) if n_pages % p == 0 and bs * (n_pages // p) <= grid_steps]
    assert fits, "prompt grid too small to carry the cache sweep"
    pps = min([p for p in fits if p >= PAGES_PER_STEP] or fits[-1:])
    steps_per_seq = n_pages // pps
    dec_steps = bs * steps_per_seq
    nrow = heads * pps

    ck = jnp.transpose(cache_k, (0, 1, 3, 4, 2)).reshape(n_attn * n_pool, d, page)
    cv = jnp.transpose(cache_v, (0, 1, 3, 4, 2)).reshape(n_attn * n_pool, d, page)
    bias_rows = jnp.broadcast_to(attn_bias[layer_a][None, :, None], (pps, heads, page)).reshape(nrow, page)

    step_ids = jnp.minimum(jnp.arange(grid_steps, dtype=jnp.int32), dec_steps - 1)
    seq_of = step_ids // steps_per_seq
    c_of = step_ids % steps_per_seq
    cols = (steps_per_seq - 1 - c_of)[:, None] * pps + jnp.arange(pps, dtype=jnp.int32)[None, :]
    pages = (layer_a * n_pool + page_table[seq_of[:, None], cols]).reshape(grid_steps * pps)

    def step_of(b, hp, qi):
        return (b * n_hp + hp) * nq + qi

    def seq_spec(b, hp, qi, pages_ref, seq_ref, c_ref):
        return seq_ref[step_of(b, hp, qi)], 0, 0

    def page_spec(i):
        return pl.BlockSpec((None, d, page),
                            lambda b, hp, qi, pages_ref, seq_ref, c_ref: (pages_ref[step_of(b, hp, qi) * pps + i], 0, 0))

    qo = pl.BlockSpec((tq, LANES), lambda b, hp, qi, *_: (b * nq + qi, hp))
    kv = pl.BlockSpec((None, LANES, seq), lambda b, hp, qi, *_: (b, hp, 0))
    grid_spec = pltpu.PrefetchScalarGridSpec(
        num_scalar_prefetch=3,
        grid=grid,
        in_specs=[pl.BlockSpec(memory_space=pltpu.SMEM), qo, kv, kv,
                  pl.BlockSpec((None, 1, d), seq_spec),
                  pl.BlockSpec((nrow, page), lambda b, hp, qi, *_: (0, 0))]
                 + [page_spec(i) for i in range(pps)] + [page_spec(i) for i in range(pps)],
        out_specs=(qo, pl.BlockSpec((None, 1, d), seq_spec)),
        scratch_shapes=[pltpu.VMEM((2, seq // ts, LANES, ts), BF16), pltpu.VMEM((seq // tq, LANES, tq), BF16),
                        pltpu.VMEM((2, tq, LANES), F32), pltpu.VMEM((2, tq, LANES), F32),
                        pltpu.VMEM((heads, d), BF16), pltpu.VMEM((nrow, page), F32),
                        pltpu.VMEM((d, page), F32), pltpu.VMEM((heads, page), F32)],
    )
    o, o_s = pl.pallas_call(
        functools.partial(_attention_kernel, pps=pps, dec_steps=dec_steps,
                          steps_per_seq=steps_per_seq, grid=grid,
                          attn=dict(layer_a=layer_a, tq=tq, ts=ts, seq=seq)),
        out_shape=(jax.ShapeDtypeStruct((n, d), BF16), jax.ShapeDtypeStruct((bs, 1, d), BF16)),
        grid_spec=grid_spec,
        compiler_params=_params("arbitrary", "arbitrary", "arbitrary"),
        name="sb_attention",
    )(pages, seq_of, c_of, attn_bias, q, kt, vt, q_s.reshape(bs, 1, d), bias_rows, *([ck] * pps), *([cv] * pps))
    return o, o_s.reshape(bs, d)


def _conv_carry_in(rows, t, prev_ref, stage_ref):
    if rows.per_row:
        return

    @pl.when(t == 0)
    def _():
        stage_ref[SUBLANES - 2:SUBLANES, :] = prev_ref[...]

    @pl.when(t != 0)
    def _():
        stage_ref[0:SUBLANES, :] = stage_ref[rows.tm:rows.tm + SUBLANES, :]


def _conv3(u, cols, conv_w_ref, rows, prev_ref, stage_ref, state_ref, width):
    tm = rows.tm
    w = conv_w_ref[:, cols]
    if rows.per_row:
        lo = slice(cols.start, cols.stop)
        hi = slice(width + cols.start, width + cols.stop)
        u2, u1 = prev_ref[:, lo], prev_ref[:, hi]
        state_ref[:, lo] = u1
        state_ref[:, hi] = u
    else:
        stage_ref[SUBLANES:SUBLANES + tm, cols] = u
        u1 = stage_ref[SUBLANES - 1:SUBLANES - 1 + tm, cols]
        u2 = stage_ref[SUBLANES - 2:SUBLANES - 2 + tm, cols]
        state_ref[:, cols] = u[tm - 2:tm]
    return w[0:1] * u2 + w[1:2] * u1 + w[2:3] * u


def _ffn_kernel(*refs, rows, d, f, has_proj, final_norm):
    it = iter(refs)
    x_ref = next(it)
    if has_proj:
        a_ref, wp_ref, g1_ref = next(it), next(it), next(it)
    sh_ref, sc_ref, g2_ref, gain_ref = next(it), next(it), next(it), next(it)
    wu_ref, cw_ref, cb_ref, wd_ref, prev_ref = next(it), next(it), next(it), next(it), next(it)
    if final_norm:
        gout_ref = next(it)
    y_ref, state_ref = next(it), next(it)
    act_ref = next(it)
    stage_ref = None if rows.per_row else next(it)
    t = pl.program_id(1)

    x = x_ref[...]
    if has_proj:
        x = x + g1_ref[...] * jnp.dot(a_ref[...], wp_ref[...], preferred_element_type=F32)
    h = _rms_mod(x, gain_ref[...], sc_ref[...], sh_ref[...]).astype(BF16)
    _conv_carry_in(rows, t, prev_ref, stage_ref)
    for c0 in range(0, f, FF_CHUNK):
        cols = slice(c0, c0 + FF_CHUNK)
        g = jnp.dot(h, wu_ref[:, cols], preferred_element_type=F32)
        u = jnp.dot(h, wu_ref[:, f + c0:f + c0 + FF_CHUNK], preferred_element_type=F32)
        a = _conv3(g, cols, cw_ref, rows, prev_ref, stage_ref, state_ref, f) + cb_ref[:, cols]
        act_ref[:, cols] = (_silu(a) * u).astype(BF16)
    x = x + g2_ref[...] * jnp.dot(act_ref[...], wd_ref[...], preferred_element_type=F32)
    if final_norm:
        x = x * lax.rsqrt(jnp.mean(x * x, axis=-1, keepdims=True) + NORM_EPS) * gout_ref[...]
    y_ref[...] = x


def _ffn(x, mod, gain, w_up, conv_w, conv_b, w_down, prev, rows, layer, proj=None, out_gain=None):
    n, d = x.shape
    f = w_down.shape[0]
    const = lambda g, t: (0, 0)
    args, specs = [x], [_x_spec(rows, d)]
    if proj is not None:
        a, w_proj = proj
        args += [a, w_proj, mod]
        specs += [_x_spec(rows, d), _const_spec((d, d), const), _mod_spec(rows, layer, 2, d)]
    args += [mod, mod, mod, gain, w_up, conv_w, conv_b.reshape(1, f), w_down, prev]
    specs += [_mod_spec(rows, layer, 3, d), _mod_spec(rows, layer, 4, d), _mod_spec(rows, layer, 5, d),
              pl.BlockSpec((None, 1, d), lambda g, t: (layer, 0, 0)),
              _const_spec((d, 2 * f), const), _const_spec((3, f), const), _const_spec((1, f), const),
              _const_spec((f, d), const), _state_in_spec(rows, f)]
    if out_gain is not None:
        args.append(out_gain.reshape(1, d))
        specs.append(_const_spec((1, d), const))
    scratch = [pltpu.VMEM((rows.tm, f), BF16)]
    if not rows.per_row:
        scratch.append(pltpu.VMEM((rows.tm + SUBLANES, f), F32))
    state_shape = (n, 2 * f) if rows.per_row else (rows.n_groups, 2, f)
    return pl.pallas_call(
        functools.partial(_ffn_kernel, rows=rows, d=d, f=f, has_proj=proj is not None,
                          final_norm=out_gain is not None),
        out_shape=(jax.ShapeDtypeStruct((n, d), F32), jax.ShapeDtypeStruct(state_shape, F32)),
        grid=(rows.n_groups, rows.tiles),
        in_specs=specs,
        out_specs=(_x_spec(rows, d), _state_in_spec(rows, f)),
        scratch_shapes=scratch,
        compiler_params=_params("arbitrary", "arbitrary"),
        name="conv_ffn",
    )(*args)


def _sconv_kernel(*refs, rows, d):
    x_ref, sh_ref, sc_ref, g1_ref, gain_ref, wi_ref, cw_ref, wo_ref, prev_ref, y_ref, state_ref, gated_ref = refs[:12]
    stage_ref = None if rows.per_row else refs[12]
    t = pl.program_id(1)
    x = x_ref[...]
    h = _rms_mod(x, gain_ref[...], sc_ref[...], sh_ref[...]).astype(BF16)
    _conv_carry_in(rows, t, prev_ref, stage_ref)
    for c0 in range(0, d, FF_CHUNK):
        cols = slice(c0, c0 + FF_CHUNK)
        b_gate = jnp.dot(h, wi_ref[:, cols], preferred_element_type=F32)
        c_gate = jnp.dot(h, wi_ref[:, d + c0:d + c0 + FF_CHUNK], preferred_element_type=F32)
        u = jnp.dot(h, wi_ref[:, 2 * d + c0:2 * d + c0 + FF_CHUNK], preferred_element_type=F32)
        y = _conv3(c_gate * u, cols, cw_ref, rows, prev_ref, stage_ref, state_ref, d)
        gated_ref[:, cols] = (b_gate * y).astype(BF16)
    y_ref[...] = x + g1_ref[...] * jnp.dot(gated_ref[...], wo_ref[...], preferred_element_type=F32)


def _short_conv(x, mod, gain, w_in, conv_w, w_out, prev, rows, layer):
    n, d = x.shape
    const = lambda g, t: (0, 0)
    scratch = [pltpu.VMEM((rows.tm, d), BF16)]
    if not rows.per_row:
        scratch.append(pltpu.VMEM((rows.tm + SUBLANES, d), F32))
    state_shape = (n, 2 * d) if rows.per_row else (rows.n_groups, 2, d)
    return pl.pallas_call(
        functools.partial(_sconv_kernel, rows=rows, d=d),
        out_shape=(jax.ShapeDtypeStruct((n, d), F32), jax.ShapeDtypeStruct(state_shape, F32)),
        grid=(rows.n_groups, rows.tiles),
        in_specs=[_x_spec(rows, d), _mod_spec(rows, layer, 0, d), _mod_spec(rows, layer, 1, d),
                  _mod_spec(rows, layer, 2, d), pl.BlockSpec((None, 1, d), lambda g, t: (layer, 0, 0)),
                  _const_spec((d, 3 * d), const), _const_spec((3, d), const), _const_spec((d, d), const),
                  _state_in_spec(rows, d)],
        out_specs=(_x_spec(rows, d), _state_in_spec(rows, d)),
        scratch_shapes=scratch,
        compiler_params=_params("arbitrary", "arbitrary"),
        name="short_conv",
    )(x, mod, mod, mod, gain, w_in, conv_w, w_out, prev)


class Group(NamedTuple):
    x: jax.Array
    mod: jax.Array
    rows: Rows
    conv_prev: jax.Array
    ffn_prev: jax.Array


def _trunk(prompt, sample, w, cache_k, cache_v, page_table):
    depth = w["norm_mix"].shape[0]
    n_mix = 2
    groups = [prompt, sample]
    new = [dict(k=[], v=[], conv=[], ffn=[]) for _ in groups]
    for i in range(depth):
        last = i == depth - 1
        projs = [None, None]
        if i % n_mix == 0:
            a = i // n_mix
            q_p, k_p, v_p = _qkv(prompt.x, prompt.mod, w["norm_mix"], w["attn_w_q"][a], w["attn_w_kv_t"][a],
                                 prompt.rows, i, LOG2E * HEAD_DIM ** -0.5, BF16, True)
            q_s, k_s, v_s = _qkv(sample.x, sample.mod, w["norm_mix"], w["attn_w_q"][a], w["attn_w_kv"][a],
                                 sample.rows, i, HEAD_DIM ** -0.5, F32, False)
            o_p, o_s = _attention(q_p, k_p, v_p, q_s, cache_k, cache_v, page_table, w["attn_bias"], a)
            for n, k, v in ((new[0], k_p, v_p), (new[1], k_s, v_s)):
                n["k"].append(k)
                n["v"].append(v)
            projs = [(o_p, w["attn_w_o"][a]), (o_s, w["attn_w_o"][a])]
        else:
            j = i // n_mix
            for gi, g in enumerate(groups):
                x, st = _short_conv(g.x, g.mod, w["norm_mix"], w["sc_w_in"][j], w["sc_conv_w"][j], w["sc_w_out"][j],
                                    g.conv_prev[j], g.rows, i)
                groups[gi] = g._replace(x=x)
                new[gi]["conv"].append(st)
        for gi, g in enumerate(groups):
            x, st = _ffn(g.x, g.mod, w["norm_ffn"], w["ffn_w_up"][i], w["ffn_conv_w"][i], w["ffn_conv_b"][i],
                         w["ffn_w_down"][i], g.ffn_prev[i], g.rows, i, proj=projs[gi],
                         out_gain=w["norm_out"] if last else None)
            groups[gi] = g._replace(x=x)
            new[gi]["ffn"].append(st)
        prompt, sample = groups
    return (prompt.x, new[0]), (sample.x, new[1])


def kernel(x_prompt, x_sample, cache_k, cache_v, state_conv, state_ffn, page_table, c_prompt, c_sample, ada_w, ada_b, norm_mix, norm_ffn, norm_out, attn_w_qkv, attn_w_o, attn_bias, sc_w_in, sc_conv_w, sc_w_out, ffn_w_up, ffn_conv_w, ffn_conv_b, ffn_w_down):
    bp, seq, d = x_prompt.shape
    bs, dec_seq, _ = x_sample.shape
    assert dec_seq == 1, "the sample group decodes one token per sequence"
    depth = ada_w.shape[0]
    f = ffn_w_down.shape[1]
    heads = d // HEAD_DIM
    n_conv = state_conv.shape[0]

    mod_p, mod_s = _ada_mod(c_prompt, c_sample, ada_w, ada_b)
    mod_p = mod_p.reshape(depth, bp, 1, N_MOD * d)

    shared = dict(
        norm_mix=norm_mix.reshape(depth, 1, d), norm_ffn=norm_ffn.reshape(depth, 1, d), norm_out=norm_out,
        attn_w_q=attn_w_qkv[:, :, :d].astype(BF16), attn_w_kv=attn_w_qkv[:, :, d:].astype(BF16),
        attn_w_kv_t=jnp.swapaxes(attn_w_qkv[:, :, d:], 1, 2).astype(BF16),
        attn_w_o=attn_w_o.astype(BF16), attn_bias=attn_bias,
        sc_w_in=sc_w_in.astype(BF16), sc_conv_w=sc_conv_w, sc_w_out=sc_w_out.astype(BF16),
        ffn_w_up=ffn_w_up.astype(BF16), ffn_conv_w=ffn_conv_w, ffn_conv_b=ffn_conv_b,
        ffn_w_down=ffn_w_down.astype(BF16))

    tm = min(ROW_TILE, seq)
    prompt = Group(x_prompt.reshape(bp * seq, d), mod_p, Rows(n_groups=bp, tiles=seq // tm, tm=tm, per_row=False),
                   jnp.zeros((n_conv, bp, 2, d), F32), jnp.zeros((depth, bp, 2, f), F32))
    sample = Group(x_sample.reshape(bs, d), mod_s, Rows(n_groups=1, tiles=1, tm=bs, per_row=True),
                   state_conv.reshape(n_conv, bs, 2 * d), state_ffn.reshape(depth, bs, 2 * f))
    (y_p, new_p), (y_s, new_s) = _trunk(prompt, sample, shared, cache_k, cache_v, page_table)

    def heads_of(ts, b, t):
        return jnp.stack(ts).reshape(len(ts), b, t, heads, HEAD_DIM)

    def heads_of_t(ts):
        return jnp.transpose(jnp.stack(ts).reshape(len(ts), bp, heads, HEAD_DIM, seq), (0, 1, 4, 2, 3))

    return (y_p.reshape(bp, seq, d), y_s.reshape(bs, 1, d),
            heads_of_t(new_p["k"]), heads_of_t(new_p["v"]), heads_of(new_s["k"], bs, 1), heads_of(new_s["v"], bs, 1),
            jnp.stack(new_p["conv"]), jnp.stack(new_s["conv"]).reshape(n_conv, bs, 2, d),
            jnp.stack(new_p["ffn"]), jnp.stack(new_s["ffn"]).reshape(depth, bs, 2, f))
```

```python
import functools
from typing import NamedTuple

import jax
import jax.numpy as jnp
from jax import lax
from jax.experimental import pallas as pl
from jax.experimental.pallas import tpu as pltpu

F32 = jnp.float32
BF16 = jnp.bfloat16

NORM_EPS = 1e-6
N_MOD = 6
HEAD_DIM = 64
LANES = 128
SUBLANES = 8
VMEM_LIMIT = 56 * 1024 * 1024

ROW_TILE = 512
FF_CHUNK = 256
ATTN_Q = 512
ATTN_S = 256
PAGES_PER_STEP = 8


class Rows(NamedTuple):
    n_groups: int
    tiles: int
    tm: int
    per_row: bool


def _params(*sem):
    return pltpu.CompilerParams(dimension_semantics=sem, vmem_limit_bytes=VMEM_LIMIT)


def _const_spec(shape, index):
    return pl.BlockSpec(shape, index, pipeline_mode=pl.Buffered(1))


def _x_spec(rows, width):
    return pl.BlockSpec((rows.tm, width), lambda g, t: (g * rows.tiles + t, 0))


def _mod_spec(rows, layer, j, d):
    if rows.per_row:
        return pl.BlockSpec((None, rows.tm, d), lambda g, t: (layer, 0, j))
    return pl.BlockSpec((None, None, 1, d), lambda g, t: (layer, g, 0, j))


def _state_in_spec(rows, width):
    if rows.per_row:
        return pl.BlockSpec((rows.tm, 2 * width), lambda g, t: (0, 0))
    return pl.BlockSpec((None, 2, width), lambda g, t: (g, 0, 0))


def _rms_mod(x, gain, scale, shift):
    y = x * lax.rsqrt(jnp.mean(x * x, axis=-1, keepdims=True) + NORM_EPS)
    return (y * gain) * (1.0 + scale) + shift


def _silu(a):
    return a / (1.0 + jnp.exp(-a))


def _ada_kernel(cp_ref, cs_ref, w_ref, b_ref, mp_ref, ms_ref):
    w = w_ref[...].astype(BF16)
    b = b_ref[...]
    for c_ref, o_ref in ((cp_ref, mp_ref), (cs_ref, ms_ref)):
        s = _silu(c_ref[...]).astype(BF16)
        o_ref[...] = jnp.dot(s, w, preferred_element_type=F32) + b


def _ada_mod(c_prompt, c_sample, ada_w, ada_b):
    depth, d, n = ada_w.shape
    tn = n // 4
    bp, bs = c_prompt.shape[0], c_sample.shape[0]
    return pl.pallas_call(
        _ada_kernel,
        out_shape=(jax.ShapeDtypeStruct((depth, bp, n), F32), jax.ShapeDtypeStruct((depth, bs, n), F32)),
        grid=(depth, n // tn),
        in_specs=[
            pl.BlockSpec((bp, d), lambda i, j: (0, 0)),
            pl.BlockSpec((bs, d), lambda i, j: (0, 0)),
            pl.BlockSpec((None, d, tn), lambda i, j: (i, 0, j)),
            pl.BlockSpec((None, 1, tn), lambda i, j: (i, 0, j)),
        ],
        out_specs=(pl.BlockSpec((None, bp, tn), lambda i, j: (i, 0, j)),
                   pl.BlockSpec((None, bs, tn), lambda i, j: (i, 0, j))),
        compiler_params=_params("arbitrary", "arbitrary"),
        name="ada_mod",
    )(c_prompt, c_sample, ada_w, ada_b.reshape(depth, 1, n))


def _qkv_kernel(x_ref, sh_ref, sc_ref, gain_ref, wq_ref, wkv_ref, q_ref, k_ref, v_ref, *, d, q_scale, time_on_lanes):
    h = _rms_mod(x_ref[...], gain_ref[...], sc_ref[...], sh_ref[...]).astype(BF16)
    q = jnp.dot(h, wq_ref[...], preferred_element_type=F32)
    q_ref[...] = (q * q_scale).astype(q_ref.dtype)
    if time_on_lanes:
        nt = (((1,), (1,)), ((), ()))
        k_ref[...] = lax.dot_general(wkv_ref[0:d, :], h, nt, preferred_element_type=F32)
        v_ref[...] = lax.dot_general(wkv_ref[d:2 * d, :], h, nt, preferred_element_type=F32)
    else:
        k_ref[...] = jnp.dot(h, wkv_ref[:, 0:d], preferred_element_type=F32)
        v_ref[...] = jnp.dot(h, wkv_ref[:, d:2 * d], preferred_element_type=F32)


def _qkv(x, mod, gain, w_q, w_kv, rows, layer, q_scale, q_dtype, time_on_lanes):
    n, d = x.shape
    row_out = pl.BlockSpec((rows.tm, d), lambda g, t: (g * rows.tiles + t, 0))
    if time_on_lanes:
        kv_shape = jax.ShapeDtypeStruct((rows.n_groups, d, rows.tiles * rows.tm), F32)
        kv_out = pl.BlockSpec((None, d, rows.tm), lambda g, t: (g, 0, t))
    else:
        kv_shape = jax.ShapeDtypeStruct((n, d), F32)
        kv_out = row_out
    return pl.pallas_call(
        functools.partial(_qkv_kernel, d=d, q_scale=q_scale, time_on_lanes=time_on_lanes),
        out_shape=(jax.ShapeDtypeStruct((n, d), q_dtype), kv_shape, kv_shape),
        grid=(rows.n_groups, rows.tiles),
        in_specs=[
            _x_spec(rows, d),
            _mod_spec(rows, layer, 0, d),
            _mod_spec(rows, layer, 1, d),
            pl.BlockSpec((None, 1, d), lambda g, t: (layer, 0, 0)),
            _const_spec((d, d), lambda g, t: (0, 0)),
            _const_spec(w_kv.shape, lambda g, t: (0, 0)),
        ],
        out_specs=(row_out, kv_out, kv_out),
        compiler_params=_params("arbitrary", "arbitrary"),
        name="qkv_proj",
    )(x, mod, mod, gain, w_q, w_kv)


LOG2E = 1.4426950408889634


def _softplus(z):
    return jnp.maximum(z, 0.0) + jnp.log(1.0 + jnp.exp(-jnp.abs(z)))


EXP2_ARG_CAP = 64.0


def _softplus2(z2):
    return jnp.maximum(jnp.log(1.0 + jnp.exp2(jnp.minimum(z2, EXP2_ARG_CAP))) * LOG2E, z2)


def _split_bf16(a):
    hi = a.astype(BF16)
    return hi, (a - hi.astype(F32)).astype(BF16)


def _strict_tri(n, later_on_rows):
    r = lax.broadcasted_iota(jnp.int32, (n, n), 0)
    c = lax.broadcasted_iota(jnp.int32, (n, n), 1)
    later = (r > c) if later_on_rows else (c > r)
    return jnp.where(later, -1.0, 0.0).astype(BF16)


def _attn_body(hp, qi, bias_ref, q_ref, kt_ref, vt_ref, o_ref, kb_ref, vb_ref, acc_ref, run_ref, *,
               layer_a, tq, ts, seq, alongside):
    nsub = tq // ts
    own_lo = (0, HEAD_DIM)
    one_lo = (HEAD_DIM, 0)

    @pl.when(qi == 0)
    def _():
        row = lax.broadcasted_iota(jnp.int32, (LANES, 1), 0)
        for h in range(2):
            own = (row >= own_lo[h]) & (row < own_lo[h] + HEAD_DIM)
            ones = jnp.where((row >= one_lo[h]) & (row < one_lo[h] + 2), 1.0, 0.0)
            for j in range(seq // ts):
                kb_ref[h, j] = jnp.where(own, kt_ref[:, j * ts:(j + 1) * ts], ones).astype(BF16)
        for j in range(seq // tq):
            vb_ref[j] = vt_ref[:, j * tq:(j + 1) * tq].astype(BF16)

    lane = lax.broadcasted_iota(jnp.int32, (1, LANES), 1)
    first = lane < HEAD_DIM
    q2 = q_ref[...].astype(F32)

    def query_with_bias(h, bias):
        own = jnp.where((lane >= own_lo[h]) & (lane < own_lo[h] + HEAD_DIM), 1.0, 0.0)
        b = jnp.full((1, LANES), bias, F32)
        b_hi = b.astype(BF16).astype(F32)
        b_lanes = jnp.where(lane == one_lo[h], b_hi, jnp.where(lane == one_lo[h] + 1, b - b_hi, 0.0))
        return (q2 * own + b_lanes).astype(BF16)

    q_head = [query_with_bias(h, bias_ref[layer_a, 2 * hp + h] * LOG2E) for h in range(2)]
    tri = _strict_tri(ts, later_on_rows=True)
    causal = lax.broadcasted_iota(jnp.int32, (ts, ts), 1) < lax.broadcasted_iota(jnp.int32, (ts, ts), 0)

    acc_ref[...] = jnp.zeros_like(acc_ref)
    run_ref[...] = jnp.zeros_like(run_ref)

    def sweep(h, r0, nrows, tile, subs):
        qh = q_head[h][r0:r0 + nrows]
        run = run_ref[h, r0:r0 + nrows, :]
        probs = []
        for kc, mask in subs:
            z = jnp.dot(qh, kb_ref[h, tile * nsub + kc], preferred_element_type=F32)
            sp = _softplus2(z)
            log_beta = z - sp
            if mask is not None:
                sp = jnp.where(mask, sp, 0.0)
            later = jnp.dot(sp.astype(BF16), tri, preferred_element_type=F32)
            p = jnp.exp2(log_beta + later + jnp.concatenate([run] * (ts // LANES), axis=1))
            if mask is not None:
                p = jnp.where(mask, p, 0.0)
            probs.append(p.astype(BF16))
            run = run + (later[:, 0:1] - sp[:, 0:1])
        lo, hi = subs[-1][0], subs[0][0] + 1
        p_all = probs[0] if len(probs) == 1 else jnp.concatenate(probs[::-1], axis=1)
        vs = vb_ref[tile, :, lo * ts:hi * ts]
        acc_ref[h, r0:r0 + nrows, :] += lax.dot_general(p_all, vs, (((1,), (1,)), ((), ())),
                                                        preferred_element_type=F32)
        run_ref[h, r0:r0 + nrows, :] = run

    for rc in range(nsub):
        subs = [(rc, causal)] + [(kc, None) for kc in reversed(range(rc))]
        for h in range(2):
            sweep(h, rc * ts, ts, qi, subs)
    alongside()

    full = [(kc, None) for kc in reversed(range(nsub))]

    def full_tile(tile):
        for h in range(2):
            sweep(h, 0, tq, tile, full)

    def body(it, carry):
        full_tile(qi - 1 - 2 * it)
        full_tile(qi - 2 - 2 * it)
        return carry

    lax.fori_loop(0, lax.shift_right_logical(qi, 1), body, 0)

    @pl.when((qi & 1) == 1)
    def _():
        full_tile(0)

    o_ref[...] = jnp.where(first, acc_ref[0], acc_ref[1]).astype(o_ref.dtype)


def _own_columns(heads, d):
    row = lax.broadcasted_iota(jnp.int32, (heads, d), 0)
    col = lax.broadcasted_iota(jnp.int32, (heads, d), 1)
    return jnp.where((col // (d // heads)) == row, 1.0, 0.0)


def _dec_start(c, q_ref, qh_ref, acc_ref, run_ref):
    @pl.when(c == 0)
    def _():
        qh_ref[...] = (_own_columns(*qh_ref.shape) * q_ref[...]).astype(BF16)
        acc_ref[...] = jnp.zeros_like(acc_ref)
        run_ref[...] = jnp.zeros_like(run_ref)


def _dec_finish(c, last_c, o_ref, acc_ref):
    @pl.when(c == last_c)
    def _():
        o_ref[...] = jnp.sum(acc_ref[...].T, axis=0, keepdims=True).astype(o_ref.dtype)


def _dec_step(bias_ref, k_refs, v_refs, qh_ref, p_ref, acc_ref, run_ref):
    pps = len(k_refs)
    d, page = k_refs[0].shape
    heads = qh_ref.shape[0]
    hd = d // heads
    q_heads = qh_ref[...]
    keys = jnp.concatenate([k_refs[i][...].astype(BF16) for i in range(pps)], axis=1)
    z_wide = jnp.dot(q_heads, keys, preferred_element_type=F32)
    z = jnp.concatenate([z_wide[:, i * page:(i + 1) * page] for i in range(pps)], axis=0) + bias_ref[...]
    sp = _softplus(z)
    tri = _strict_tri(page, later_on_rows=True)
    sp_hi, sp_lo = _split_bf16(sp)
    later = jnp.dot(sp_hi, tri, preferred_element_type=F32) + jnp.dot(sp_lo, tri, preferred_element_type=F32)

    total = jnp.sum(sp, axis=-1, keepdims=True)
    run = run_ref[...]
    later_pages = [None] * pps
    for i in reversed(range(pps)):
        later_pages[i] = run
        run = run - total[i * heads:(i + 1) * heads]
    run_ref[...] = run
    p_ref[...] = jnp.exp((z - sp) + later + jnp.concatenate(later_pages, axis=0))

    for h in range(heads):
        hs = slice(h * hd, (h + 1) * hd)
        part = None
        for i in range(pps):
            r = i * heads + h
            term = p_ref[r:r + 1, :] * v_refs[i][hs, :]
            part = term if part is None else part + term
        acc_ref[hs, :] += part


def _attention_kernel(pages_ref, seq_ref, c_ref, bias_ref, q_ref, kt_ref, vt_ref, qs_ref, bias_rows_ref, *refs,
                      pps, dec_steps, steps_per_seq, grid, attn):
    k_refs, v_refs = refs[:pps], refs[pps:2 * pps]
    o_ref, os_ref, kb_ref, vb_ref, acc_ref, run_ref, qh_ref, p_ref, dacc_ref, drun_ref = refs[2 * pps:]
    hp, qi = pl.program_id(1), pl.program_id(2)
    step = (pl.program_id(0) * grid[1] + hp) * grid[2] + qi

    c = c_ref[step]
    every_step = dec_steps == grid[0] * grid[1] * grid[2]
    active = None if every_step else step < dec_steps

    def guarded(fn):
        return fn() if every_step else pl.when(active)(fn)

    def decode():
        guarded(lambda: _dec_step(bias_rows_ref, k_refs, v_refs, qh_ref, p_ref, dacc_ref, drun_ref))

    guarded(lambda: _dec_start(c, qs_ref, qh_ref, dacc_ref, drun_ref))
    _attn_body(hp, qi, bias_ref, q_ref, kt_ref, vt_ref, o_ref, kb_ref, vb_ref, acc_ref, run_ref,
               alongside=decode, **attn)
    guarded(lambda: _dec_finish(c, steps_per_seq - 1, os_ref, dacc_ref))


def _attention(q, kt, vt, q_s, cache_k, cache_v, page_table, attn_bias, layer_a):
    n, d = q.shape
    batch, _, seq = kt.shape
    bs = q_s.shape[0]
    n_attn, n_pool, page, heads, _ = cache_k.shape
    assert page == LANES, "one cache page must fill the lane axis"
    tq = min(ATTN_Q, seq)
    ts = min(ATTN_S, tq)
    nq = seq // tq
    n_hp = d // LANES
    grid = (batch, n_hp, nq)
    grid_steps = batch * n_hp * nq

    n_pages = page_table.shape[1]
    fits = [p for p in range(1, n_pages + 1) if n_pages % p == 0 and bs * (n_pages // p) <= grid_steps]
    assert fits, "prompt grid too small to carry the cache sweep"
    pps = min([p for p in fits if p >= PAGES_PER_STEP] or fits[-1:])
    steps_per_seq = n_pages // pps
    dec_steps = bs * steps_per_seq
    nrow = heads * pps

    ck = jnp.transpose(cache_k, (0, 1, 3, 4, 2)).reshape(n_attn * n_pool, d, page)
    cv = jnp.transpose(cache_v, (0, 1, 3, 4, 2)).reshape(n_attn * n_pool, d, page)
    bias_rows = jnp.broadcast_to(attn_bias[layer_a][None, :, None], (pps, heads, page)).reshape(nrow, page)

    step_ids = jnp.minimum(jnp.arange(grid_steps, dtype=jnp.int32), dec_steps - 1)
    seq_of = step_ids // steps_per_seq
    c_of = step_ids % steps_per_seq
    cols = (steps_per_seq - 1 - c_of)[:, None] * pps + jnp.arange(pps, dtype=jnp.int32)[None, :]
    pages = (layer_a * n_pool + page_table[seq_of[:, None], cols]).reshape(grid_steps * pps)

    def step_of(b, hp, qi):
        return (b * n_hp + hp) * nq + qi

    def seq_spec(b, hp, qi, pages_ref, seq_ref, c_ref):
        return seq_ref[step_of(b, hp, qi)], 0, 0

    def page_spec(i):
        return pl.BlockSpec((None, d, page),
                            lambda b, hp, qi, pages_ref, seq_ref, c_ref: (pages_ref[step_of(b, hp, qi) * pps + i], 0, 0))

    qo = pl.BlockSpec((tq, LANES), lambda b, hp, qi, *_: (b * nq + qi, hp))
    kv = pl.BlockSpec((None, LANES, seq), lambda b, hp, qi, *_: (b, hp, 0))
    grid_spec = pltpu.PrefetchScalarGridSpec(
        num_scalar_prefetch=3,
        grid=grid,
        in_specs=[pl.BlockSpec(memory_space=pltpu.SMEM), qo, kv, kv,
                  pl.BlockSpec((None, 1, d), seq_spec),
                  pl.BlockSpec((nrow, page), lambda b, hp, qi, *_: (0, 0))]
                 + [page_spec(i) for i in range(pps)] + [page_spec(i) for i in range(pps)],
        out_specs=(qo, pl.BlockSpec((None, 1, d), seq_spec)),
        scratch_shapes=[pltpu.VMEM((2, seq // ts, LANES, ts), BF16), pltpu.VMEM((seq // tq, LANES, tq), BF16),
                        pltpu.VMEM((2, tq, LANES), F32), pltpu.VMEM((2, tq, LANES), F32),
                        pltpu.VMEM((heads, d), BF16), pltpu.VMEM((nrow, page), F32),
                        pltpu.VMEM((d, page), F32), pltpu.VMEM((heads, page), F32)],
    )
    o, o_s = pl.pallas_call(
        functools.partial(_attention_kernel, pps=pps, dec_steps=dec_steps,
                          steps_per_seq=steps_per_seq, grid=grid,
                          attn=dict(layer_a=layer_a, tq=tq, ts=ts, seq=seq)),
        out_shape=(jax.ShapeDtypeStruct((n, d), BF16), jax.ShapeDtypeStruct((bs, 1, d), BF16)),
        grid_spec=grid_spec,
        compiler_params=_params("arbitrary", "arbitrary", "arbitrary"),
        name="sb_attention",
    )(pages, seq_of, c_of, attn_bias, q, kt, vt, q_s.reshape(bs, 1, d), bias_rows, *([ck] * pps), *([cv] * pps))
    return o, o_s.reshape(bs, d)


def _conv_carry_in(rows, t, prev_ref, stage_ref):
    if rows.per_row:
        return

    @pl.when(t == 0)
    def _():
        stage_ref[SUBLANES - 2:SUBLANES, :] = prev_ref[...]

    @pl.when(t != 0)
    def _():
        stage_ref[0:SUBLANES, :] = stage_ref[rows.tm:rows.tm + SUBLANES, :]


def _conv3(u, cols, conv_w_ref, rows, prev_ref, stage_ref, state_ref, width):
    tm = rows.tm
    w = conv_w_ref[:, cols]
    if rows.per_row:
        lo = slice(cols.start, cols.stop)
        hi = slice(width + cols.start, width + cols.stop)
        u2, u1 = prev_ref[:, lo], prev_ref[:, hi]
        state_ref[:, lo] = u1
        state_ref[:, hi] = u
    else:
        stage_ref[SUBLANES:SUBLANES + tm, cols] = u
        u1 = stage_ref[SUBLANES - 1:SUBLANES - 1 + tm, cols]
        u2 = stage_ref[SUBLANES - 2:SUBLANES - 2 + tm, cols]
        state_ref[:, cols] = u[tm - 2:tm]
    return w[0:1] * u2 + w[1:2] * u1 + w[2:3] * u


def _ffn_kernel(*refs, rows, d, f, has_proj, has_mixer, final_norm):
    it = iter(refs)
    x_ref = next(it)
    if has_proj:
        a_ref, wp_ref, g1_ref = next(it), next(it), next(it)
    if has_mixer:
        msh_ref, msc_ref, mg_ref, mgain_ref = next(it), next(it), next(it), next(it)
        wi_ref, mcw_ref, wo_ref, mprev_ref = next(it), next(it), next(it), next(it)
    sh_ref, sc_ref, g2_ref, gain_ref = next(it), next(it), next(it), next(it)
    wu_ref, cw_ref, cb_ref, wd_ref, prev_ref = next(it), next(it), next(it), next(it), next(it)
    if final_norm:
        gout_ref = next(it)
    y_ref, state_ref = next(it), next(it)
    mstate_ref = next(it) if has_mixer else None
    act_ref = next(it)
    stage_ref = None if rows.per_row else next(it)
    if has_mixer:
        gated_ref = next(it)
        mstage_ref = None if rows.per_row else next(it)
    t = pl.program_id(1)

    x = x_ref[...]
    if has_proj:
        x = x + g1_ref[...] * jnp.dot(a_ref[...], wp_ref[...], preferred_element_type=F32)
    if has_mixer:
        hm = _rms_mod(x, mgain_ref[...], msc_ref[...], msh_ref[...]).astype(BF16)
        _conv_carry_in(rows, t, mprev_ref, mstage_ref)
        for c0 in range(0, d, FF_CHUNK):
            cols = slice(c0, c0 + FF_CHUNK)
            b_gate = jnp.dot(hm, wi_ref[:, cols], preferred_element_type=F32)
            c_gate = jnp.dot(hm, wi_ref[:, d + c0:d + c0 + FF_CHUNK], preferred_element_type=F32)
            u = jnp.dot(hm, wi_ref[:, 2 * d + c0:2 * d + c0 + FF_CHUNK], preferred_element_type=F32)
            y = _conv3(c_gate * u, cols, mcw_ref, rows, mprev_ref, mstage_ref, mstate_ref, d)
            gated_ref[:, cols] = (b_gate * y).astype(BF16)
        x = x + mg_ref[...] * jnp.dot(gated_ref[...], wo_ref[...], preferred_element_type=F32)
    h = _rms_mod(x, gain_ref[...], sc_ref[...], sh_ref[...]).astype(BF16)
    _conv_carry_in(rows, t, prev_ref, stage_ref)
    for c0 in range(0, f, FF_CHUNK):
        cols = slice(c0, c0 + FF_CHUNK)
        g = jnp.dot(h, wu_ref[:, cols], preferred_element_type=F32)
        u = jnp.dot(h, wu_ref[:, f + c0:f + c0 + FF_CHUNK], preferred_element_type=F32)
        a = _conv3(g, cols, cw_ref, rows, prev_ref, stage_ref, state_ref, f) + cb_ref[:, cols]
        act_ref[:, cols] = (_silu(a) * u).astype(BF16)
    x = x + g2_ref[...] * jnp.dot(act_ref[...], wd_ref[...], preferred_element_type=F32)
    if final_norm:
        x = x * lax.rsqrt(jnp.mean(x * x, axis=-1, keepdims=True) + NORM_EPS) * gout_ref[...]
    y_ref[...] = x


def _ffn(x, mod, gain, w_up, conv_w, conv_b, w_down, prev, rows, layer, proj=None, mixer=None, out_gain=None):
    n, d = x.shape
    f = w_down.shape[0]
    const = lambda g, t: (0, 0)
    args, specs = [x], [_x_spec(rows, d)]
    if proj is not None:
        a, w_proj = proj
        args += [a, w_proj, mod]
        specs += [_x_spec(rows, d), _const_spec((d, d), const), _mod_spec(rows, layer, 2, d)]
    if mixer is not None:
        m_gain, w_in, m_conv_w, w_out, m_prev = mixer
        args += [mod, mod, mod, m_gain, w_in, m_conv_w, w_out, m_prev]
        specs += [_mod_spec(rows, layer, 0, d), _mod_spec(rows, layer, 1, d), _mod_spec(rows, layer, 2, d),
                  pl.BlockSpec((None, 1, d), lambda g, t: (layer, 0, 0)),
                  _const_spec((d, 3 * d), const), _const_spec((3, d), const), _const_spec((d, d), const),
                  _state_in_spec(rows, d)]
    args += [mod, mod, mod, gain, w_up, conv_w, conv_b.reshape(1, f), w_down, prev]
    specs += [_mod_spec(rows, layer, 3, d), _mod_spec(rows, layer, 4, d), _mod_spec(rows, layer, 5, d),
              pl.BlockSpec((None, 1, d), lambda g, t: (layer, 0, 0)),
              _const_spec((d, 2 * f), const), _const_spec((3, f), const), _const_spec((1, f), const),
              _const_spec((f, d), const), _state_in_spec(rows, f)]
    if out_gain is not None:
        args.append(out_gain.reshape(1, d))
        specs.append(_const_spec((1, d), const))
    def conv_scratch(width):
        stage = [] if rows.per_row else [pltpu.VMEM((rows.tm + SUBLANES, width), F32)]
        return [pltpu.VMEM((rows.tm, width), BF16)] + stage

    def state_shape(width):
        return jax.ShapeDtypeStruct((n, 2 * width) if rows.per_row else (rows.n_groups, 2, width), F32)

    widths = [f] + ([d] if mixer is not None else [])
    return pl.pallas_call(
        functools.partial(_ffn_kernel, rows=rows, d=d, f=f, has_proj=proj is not None,
                          has_mixer=mixer is not None, final_norm=out_gain is not None),
        out_shape=(jax.ShapeDtypeStruct((n, d), F32), *[state_shape(w) for w in widths]),
        grid=(rows.n_groups, rows.tiles),
        in_specs=specs,
        out_specs=(_x_spec(rows, d), *[_state_in_spec(rows, w) for w in widths]),
        scratch_shapes=[s for w in widths for s in conv_scratch(w)],
        compiler_params=_params("arbitrary", "arbitrary"),
        name="conv_ffn",
    )(*args)


class Group(NamedTuple):
    x: jax.Array
    mod: jax.Array
    rows: Rows
    conv_prev: jax.Array
    ffn_prev: jax.Array


def _trunk(prompt, sample, w, cache_k, cache_v, page_table):
    depth = w["norm_mix"].shape[0]
    n_mix = 2
    groups = [prompt, sample]
    new = [dict(k=[], v=[], conv=[], ffn=[]) for _ in groups]
    for i in range(depth):
        last = i == depth - 1
        projs = [None, None]
        if i % n_mix == 0:
            a = i // n_mix
            q_p, k_p, v_p = _qkv(prompt.x, prompt.mod, w["norm_mix"], w["attn_w_q"][a], w["attn_w_kv_t"][a],
                                 prompt.rows, i, LOG2E * HEAD_DIM ** -0.5, BF16, True)
            q_s, k_s, v_s = _qkv(sample.x, sample.mod, w["norm_mix"], w["attn_w_q"][a], w["attn_w_kv"][a],
                                 sample.rows, i, HEAD_DIM ** -0.5, F32, False)
            o_p, o_s = _attention(q_p, k_p, v_p, q_s, cache_k, cache_v, page_table, w["attn_bias"], a)
            for n, k, v in ((new[0], k_p, v_p), (new[1], k_s, v_s)):
                n["k"].append(k)
                n["v"].append(v)
            projs = [(o_p, w["attn_w_o"][a]), (o_s, w["attn_w_o"][a])]
        for gi, g in enumerate(groups):
            mixer = None
            if i % n_mix != 0:
                j = i // n_mix
                mixer = (w["norm_mix"], w["sc_w_in"][j], w["sc_conv_w"][j], w["sc_w_out"][j], g.conv_prev[j])
            x, st, *mixer_state = _ffn(g.x, g.mod, w["norm_ffn"], w["ffn_w_up"][i], w["ffn_conv_w"][i],
                                       w["ffn_conv_b"][i], w["ffn_w_down"][i], g.ffn_prev[i], g.rows, i,
                                       proj=projs[gi], mixer=mixer, out_gain=w["norm_out"] if last else None)
            groups[gi] = g._replace(x=x)
            new[gi]["ffn"].append(st)
            new[gi]["conv"] += mixer_state
        prompt, sample = groups
    return (prompt.x, new[0]), (sample.x, new[1])


def kernel(x_prompt, x_sample, cache_k, cache_v, state_conv, state_ffn, page_table, c_prompt, c_sample, ada_w, ada_b, norm_mix, norm_ffn, norm_out, attn_w_qkv, attn_w_o, attn_bias, sc_w_in, sc_conv_w, sc_w_out, ffn_w_up, ffn_conv_w, ffn_conv_b, ffn_w_down):
    bp, seq, d = x_prompt.shape
    bs, dec_seq, _ = x_sample.shape
    assert dec_seq == 1, "the sample group decodes one token per sequence"
    depth = ada_w.shape[0]
    f = ffn_w_down.shape[1]
    heads = d // HEAD_DIM
    n_conv = state_conv.shape[0]

    mod_p, mod_s = _ada_mod(c_prompt, c_sample, ada_w, ada_b)
    mod_p = mod_p.reshape(depth, bp, 1, N_MOD * d)

    shared = dict(
        norm_mix=norm_mix.reshape(depth, 1, d), norm_ffn=norm_ffn.reshape(depth, 1, d), norm_out=norm_out,
        attn_w_q=attn_w_qkv[:, :, :d].astype(BF16), attn_w_kv=attn_w_qkv[:, :, d:].astype(BF16),
        attn_w_kv_t=jnp.swapaxes(attn_w_qkv[:, :, d:], 1, 2).astype(BF16),
        attn_w_o=attn_w_o.astype(BF16), attn_bias=attn_bias,
        sc_w_in=sc_w_in.astype(BF16), sc_conv_w=sc_conv_w, sc_w_out=sc_w_out.astype(BF16),
        ffn_w_up=ffn_w_up.astype(BF16), ffn_conv_w=ffn_conv_w, ffn_conv_b=ffn_conv_b,
        ffn_w_down=ffn_w_down.astype(BF16))

    tm = min(ROW_TILE, seq)
    prompt = Group(x_prompt.reshape(bp * seq, d), mod_p, Rows(n_groups=bp, tiles=seq // tm, tm=tm, per_row=False),
                   jnp.zeros((n_conv, bp, 2, d), F32), jnp.zeros((depth, bp, 2, f), F32))
    sample = Group(x_sample.reshape(bs, d), mod_s, Rows(n_groups=1, tiles=1, tm=bs, per_row=True),
                   state_conv.reshape(n_conv, bs, 2 * d), state_ffn.reshape(depth, bs, 2 * f))
    (y_p, new_p), (y_s, new_s) = _trunk(prompt, sample, shared, cache_k, cache_v, page_table)

    def heads_of(ts, b, t):
        return jnp.stack(ts).reshape(len(ts), b, t, heads, HEAD_DIM)

    def heads_of_t(ts):
        return jnp.transpose(jnp.stack(ts).reshape(len(ts), bp, heads, HEAD_DIM, seq), (0, 1, 4, 2, 3))

    return (y_p.reshape(bp, seq, d), y_s.reshape(bs, 1, d),
            heads_of_t(new_p["k"]), heads_of_t(new_p["v"]), heads_of(new_s["k"], bs, 1), heads_of(new_s["v"], bs, 1),
            jnp.stack(new_p["conv"]), jnp.stack(new_s["conv"]).reshape(n_conv, bs, 2, d),
            jnp.stack(new_p["ffn"]), jnp.stack(new_s["ffn"]).reshape(depth, bs, 2, f))
```
